```python
import math
import jax
import jax.numpy as jnp
from jax import lax
import numpy as np

D_MODEL = 1024
BATCH = 8
SEQ = 2048
DEPTH = 2
DEC_BATCH = 32
DEC_SEQ = 8
PAST_LEN = 8192
PAGE_SIZE = 128

N_HEADS = 16
HEAD_DIM = D_MODEL // N_HEADS
D_FF = 2816
MOBA_BLOCK = 256
MOBA_TOPK = 3
MOBA_Q_CHUNK = 16
FOX_Q_BLOCK = 128
N_MOBA_LAYERS = (DEPTH + 1) // 2
N_FOX_LAYERS = DEPTH // 2
DN_ALPHA = (2.0 * DEPTH) ** 0.25
DN_BETA = (8.0 * DEPTH) ** -0.25
LN_EPS = 1e-5
NEG = -1e30

kernel_name = 'moba_fox_interleaved_macaron_deepnorm_step'


def _alibi_slopes():
    return jnp.asarray(2.0 ** (-8.0 * np.arange(1, N_HEADS + 1) / N_HEADS), jnp.float32)


def _layer_norm(x, g, b):
    xf = x.astype(jnp.float32)
    mu = jnp.mean(xf, axis=-1, keepdims=True)
    var = jnp.mean(jnp.square(xf - mu), axis=-1, keepdims=True)
    y = (xf - mu) * lax.rsqrt(var + LN_EPS) * g.astype(jnp.float32) + b.astype(jnp.float32)
    return y.astype(x.dtype)


def _post_norm(x, sub, g, b):
    return _layer_norm(DN_ALPHA * x + sub, g, b)


def _swiglu(x, w_up, w_down):
    gate, up = jnp.split(x @ w_up, 2, axis=-1)
    return (jax.nn.silu(gate) * up) @ w_down


def _heads(z):
    return z.reshape(z.shape[:-1] + (N_HEADS, HEAD_DIM))


def _moba_proj(x, w_in):
    q, k, v = jnp.split(x @ w_in, 3, axis=-1)
    return _heads(q), _heads(k), _heads(v)


def _fox_proj(x, w_in, b_f):
    z = x @ w_in
    dm = D_MODEL
    logf = jax.nn.log_sigmoid(z[..., 3 * dm:].astype(jnp.float32) + b_f.astype(jnp.float32))
    return _heads(z[..., :dm]), _heads(z[..., dm:2 * dm]), _heads(z[..., 2 * dm:3 * dm]), logf


def _select_blocks(gate, q_pos, n_cand_max):
    ksel = min(MOBA_TOPK, n_cand_max)
    q_blk = q_pos // MOBA_BLOCK
    if ksel == 0:
        idx = jnp.zeros(gate.shape[:-1] + (0,), jnp.int32)
    else:
        cand = jnp.arange(gate.shape[-1])[None, :] < q_blk[:, None]
        _, idx = lax.top_k(jnp.where(cand[None, :, None, :], gate, NEG), ksel)
    idx = idx.astype(jnp.int32)
    ok = idx < q_blk[None, :, None, None]
    return idx, ok


def _block_positions(idx, ok):
    off = jnp.arange(MOBA_BLOCK, dtype=jnp.int32)
    n = idx.shape[-1] * MOBA_BLOCK
    pos = (idx[..., None] * MOBA_BLOCK + off).reshape(idx.shape[:-1] + (n,))
    return pos, jnp.repeat(ok, MOBA_BLOCK, axis=-1)


def _moba_core(q, q_pos, kp, vp, kp_pos, kp_ok, ko, vo, ko_pos, slopes):
    scale = HEAD_DIM ** -0.5
    f32 = jnp.float32
    sl = slopes[None, None, :, None]
    dist_p = (q_pos[None, :, None, None] - kp_pos).astype(f32)
    s_p = jnp.einsum('bqhd,bqhpd->bqhp', q, kp, preferred_element_type=f32) * scale - sl * dist_p
    s_p = jnp.where(kp_ok, s_p, NEG)
    dist_o = q_pos[:, None] - ko_pos[None, :]
    s_o = jnp.einsum('bqhd,bohd->bqho', q, ko, preferred_element_type=f32) * scale - sl * dist_o.astype(f32)[None, :, None, :]
    s_o = jnp.where((dist_o >= 0)[None, :, None, :], s_o, NEG)
    n_p = s_p.shape[-1]
    p = jax.nn.softmax(jnp.concatenate([s_p, s_o], axis=-1), axis=-1)
    out = (jnp.einsum('bqhp,bqhpd->bqhd', p[..., :n_p].astype(vp.dtype), vp, preferred_element_type=f32)
           + jnp.einsum('bqho,bohd->bqhd', p[..., n_p:].astype(vo.dtype), vo, preferred_element_type=f32))
    return out.astype(q.dtype)


def _moba_prompt(q, k, v, slopes):
    b, s, h, d = q.shape
    nblk = -(-s // MOBA_BLOCK)
    pad = nblk * MOBA_BLOCK - s
    kpad = jnp.pad(k, ((0, 0), (0, pad), (0, 0), (0, 0)))
    vpad = jnp.pad(v, ((0, 0), (0, pad), (0, 0), (0, 0)))
    kb = kpad.reshape(b, nblk, MOBA_BLOCK, h, d)
    vb = vpad.reshape(b, nblk, MOBA_BLOCK, h, d)
    kmean = jnp.sum(kb, axis=2, dtype=jnp.float32) / MOBA_BLOCK
    pos = jnp.arange(s, dtype=jnp.int32)
    gate = jnp.einsum('bthd,bnhd->bthn', q.astype(jnp.float32), kmean)
    idx, ok = _select_blocks(gate, pos, nblk - 1)
    ksel = idx.shape[-1]
    bi = jnp.arange(b)[:, None, None, None]
    hi = jnp.arange(h)[None, None, :, None]

    def chunk(ci):
        q0 = ci * MOBA_Q_CHUNK
        qc = lax.dynamic_slice_in_dim(q, q0, MOBA_Q_CHUNK, 1)
        ic = lax.dynamic_slice_in_dim(idx, q0, MOBA_Q_CHUNK, 1)
        oc = lax.dynamic_slice_in_dim(ok, q0, MOBA_Q_CHUNK, 1)
        kp = kb[bi, ic, :, hi, :].reshape(b, MOBA_Q_CHUNK, h, ksel * MOBA_BLOCK, d)
        vp = vb[bi, ic, :, hi, :].reshape(b, MOBA_Q_CHUNK, h, ksel * MOBA_BLOCK, d)
        kp_pos, kp_ok = _block_positions(ic, oc)
        own0 = (q0 // MOBA_BLOCK) * MOBA_BLOCK
        ko = lax.dynamic_slice_in_dim(kpad, own0, MOBA_BLOCK, 1)
        vo = lax.dynamic_slice_in_dim(vpad, own0, MOBA_BLOCK, 1)
        q_pos = q0 + jnp.arange(MOBA_Q_CHUNK, dtype=jnp.int32)
        ko_pos = own0 + jnp.arange(MOBA_BLOCK, dtype=jnp.int32)
        return _moba_core(qc, q_pos, kp, vp, kp_pos, kp_ok, ko, vo, ko_pos, slopes)

    out = lax.map(chunk, jnp.arange(s // MOBA_Q_CHUNK))
    return jnp.moveaxis(out, 0, 1).reshape(b, s, h, d)


def _moba_sample(q, k_new, v_new, cache_k, cache_v, li, page_table, slopes):
    b, t, h, d = q.shape
    nfull = PAST_LEN // MOBA_BLOCK
    ppb = MOBA_BLOCK // PAGE_SIZE
    q_pos = PAST_LEN + jnp.arange(t, dtype=jnp.int32)
    if nfull > 0:
        cols = page_table[:, :nfull * ppb].T
        psum = lax.map(lambda pc: jnp.sum(cache_k[li, pc], axis=1, dtype=jnp.float32), cols)
        kmean = jnp.moveaxis(psum.reshape(nfull, ppb, b, h, d).sum(axis=1), 0, 1) / MOBA_BLOCK
    else:
        kmean = jnp.zeros((b, 0, h, d), jnp.float32)
    gate = jnp.einsum('bthd,bnhd->bthn', q.astype(jnp.float32), kmean)
    idx, ok = _select_blocks(gate, q_pos, nfull)
    ksel = idx.shape[-1]
    own0 = nfull * MOBA_BLOCK
    n_own_past = PAST_LEN - own0
    own_pages = page_table[:, own0 // PAGE_SIZE:PAST_LEN // PAGE_SIZE]
    ko = jnp.concatenate([cache_k[li, own_pages].reshape(b, n_own_past, h, d), k_new], axis=1)
    vo = jnp.concatenate([cache_v[li, own_pages].reshape(b, n_own_past, h, d), v_new], axis=1)
    ko_pos = own0 + jnp.arange(n_own_past + t, dtype=jnp.int32)
    bi = jnp.arange(b)[:, None, None, None, None]
    hi = jnp.arange(h)[None, None, :, None, None]
    poff = jnp.arange(ppb, dtype=jnp.int32)

    def one(ti):
        qt = lax.dynamic_slice_in_dim(q, ti, 1, 1)
        it = lax.dynamic_slice_in_dim(idx, ti, 1, 1)
        ot = lax.dynamic_slice_in_dim(ok, ti, 1, 1)
        phys = page_table[bi, it[..., None] * ppb + poff]
        kp = cache_k[li, phys, :, hi, :].reshape(b, 1, h, ksel * MOBA_BLOCK, d)
        vp = cache_v[li, phys, :, hi, :].reshape(b, 1, h, ksel * MOBA_BLOCK, d)
        kp_pos, kp_ok = _block_positions(it, ot)
        qp = lax.dynamic_slice_in_dim(q_pos, ti, 1)
        return _moba_core(qt, qp, kp, vp, kp_pos, kp_ok, ko, vo, ko_pos, slopes)

    out = lax.map(one, jnp.arange(t))
    return jnp.moveaxis(out[:, :, 0], 0, 1)


def _fox_prompt(q, k, v, logf):
    b, s, h, d = q.shape
    scale = HEAD_DIM ** -0.5
    ct = jnp.swapaxes(lax.cumsum(logf, axis=1), 1, 2)
    k_pos = jnp.arange(s, dtype=jnp.int32)

    def blk(bi_):
        q0 = bi_ * FOX_Q_BLOCK
        qb = lax.dynamic_slice_in_dim(q, q0, FOX_Q_BLOCK, 1)
        cq = lax.dynamic_slice_in_dim(ct, q0, FOX_Q_BLOCK, 2)
        sc = (jnp.einsum('bqhd,bkhd->bhqk', qb, k, preferred_element_type=jnp.float32) * scale
              + cq[..., None] - ct[:, :, None, :])
        q_pos = q0 + jnp.arange(FOX_Q_BLOCK, dtype=jnp.int32)
        sc = jnp.where((k_pos[None, :] <= q_pos[:, None])[None, None], sc, NEG)
        p = jax.nn.softmax(sc, axis=-1)
        return jnp.einsum('bhqk,bkhd->bqhd', p.astype(v.dtype), v, preferred_element_type=jnp.float32).astype(q.dtype)

    out = lax.map(blk, jnp.arange(s // FOX_Q_BLOCK))
    return jnp.moveaxis(out, 0, 1).reshape(b, s, h, d)


def _fox_sample(q, k_new, v_new, logf_new, cache_k, cache_v, cache_logf, li, page_table):
    b, t, h, d = q.shape
    f32 = jnp.float32
    n_pages = PAST_LEN // PAGE_SIZE
    scale = HEAD_DIM ** -0.5
    lf_past = cache_logf[li, page_table].astype(f32).reshape(b, n_pages * PAGE_SIZE, h)
    suffix = lax.cumsum(lf_past, axis=1, reverse=True) - lf_past
    suf_pages = jnp.transpose(suffix.reshape(b, n_pages, PAGE_SIZE, h), (1, 0, 3, 2))
    c_new = jnp.swapaxes(lax.cumsum(logf_new, axis=1), 1, 2)

    def merge(carry, s, vals):
        m, l, acc = carry
        m_new = jnp.maximum(m, jnp.max(s, axis=-1))
        corr = jnp.exp(m - m_new)
        p = jnp.exp(s - m_new[..., None])
        l = l * corr + jnp.sum(p, axis=-1)
        acc = acc * corr[..., None] + jnp.einsum('bhtp,bphd->bhtd', p.astype(vals.dtype), vals, preferred_element_type=f32)
        return (m_new, l, acc)

    def step(carry, xs):
        cols, suf = xs
        kp = cache_k[li, cols]
        vp = cache_v[li, cols]
        s = (jnp.einsum('bthd,bphd->bhtp', q, kp, preferred_element_type=f32) * scale
             + c_new[..., None] + suf[:, :, None, :])
        return merge(carry, s, vp), None

    init = (jnp.full((b, h, t), NEG, f32), jnp.zeros((b, h, t), f32), jnp.zeros((b, h, t, d), f32))
    carry, _ = lax.scan(step, init, (page_table.T, suf_pages))
    s_n = (jnp.einsum('bthd,bshd->bhts', q, k_new, preferred_element_type=f32) * scale
           + c_new[..., :, None] - c_new[..., None, :])
    causal = jnp.arange(t)[:, None] >= jnp.arange(t)[None, :]
    s_n = jnp.where(causal, s_n, NEG)
    _, l, acc = merge(carry, s_n, v_new)
    return jnp.swapaxes(acc / l[..., None], 1, 2).astype(q.dtype)


def setup_inputs(seed: int = 0) -> dict:
    key = jax.random.key(seed)
    ks = jax.random.split(key, 20)
    f32 = jnp.float32
    d = D_MODEL
    n_pages = PAST_LEN // PAGE_SIZE
    n_pool = (DEC_BATCH * n_pages * 5) // 4
    kv_a = (N_MOBA_LAYERS, n_pool, PAGE_SIZE, N_HEADS, HEAD_DIM)
    kv_b = (N_FOX_LAYERS, n_pool, PAGE_SIZE, N_HEADS, HEAD_DIM)
    s_in = d ** -0.5
    col_a = jnp.concatenate([jnp.ones((2 * d,), f32), jnp.full((d,), DN_BETA, f32)])
    col_b = jnp.concatenate([jnp.ones((2 * d,), f32), jnp.full((d,), DN_BETA, f32), jnp.ones((N_HEADS,), f32)])
    perm = jax.random.permutation(ks[7], n_pool)[:DEC_BATCH * n_pages]
    return {
        'x_prompt': jax.random.normal(ks[0], (BATCH, SEQ, d), f32),
        'x_sample': jax.random.normal(ks[1], (DEC_BATCH, DEC_SEQ, d), f32),
        'cache_k_moba': jax.random.normal(ks[2], kv_a, f32),
        'cache_v_moba': jax.random.normal(ks[3], kv_a, f32),
        'cache_k_fox': jax.random.normal(ks[4], kv_b, f32),
        'cache_v_fox': jax.random.normal(ks[5], kv_b, f32),
        'cache_logf_fox': jax.nn.log_sigmoid(jax.random.uniform(ks[6], (N_FOX_LAYERS, n_pool, PAGE_SIZE, N_HEADS), f32, 1.0, 5.0)),
        'page_table': perm.reshape(DEC_BATCH, n_pages).astype(jnp.int32),
        'w_in_moba': jax.random.normal(ks[8], (N_MOBA_LAYERS, d, 3 * d), f32) * s_in * col_a,
        'w_out_moba': jax.random.normal(ks[9], (N_MOBA_LAYERS, d, d), f32) * (s_in * DN_BETA),
        'w_in_fox': jax.random.normal(ks[10], (N_FOX_LAYERS, d, 3 * d + N_HEADS), f32) * s_in * col_b,
        'b_f_fox': jax.random.uniform(ks[11], (N_FOX_LAYERS, N_HEADS), f32, 1.0, 5.0),
        'w_out_fox': jax.random.normal(ks[12], (N_FOX_LAYERS, d, d), f32) * (s_in * DN_BETA),
        'w_ffn_up': jax.random.normal(ks[13], (DEPTH, 2, d, 2 * D_FF), f32) * s_in,
        'w_ffn_down': jax.random.normal(ks[14], (DEPTH, 2, D_FF, d), f32) * (D_FF ** -0.5 * DN_BETA),
        'ln_g': 1.0 + 0.02 * jax.random.normal(ks[15], (DEPTH, 3, d), f32),
        'ln_b': 0.02 * jax.random.normal(ks[16], (DEPTH, 3, d), f32),
    }


def reference(x_prompt, x_sample, cache_k_moba, cache_v_moba, cache_k_fox, cache_v_fox, cache_logf_fox,
              page_table, w_in_moba, w_out_moba, w_in_fox, b_f_fox, w_out_fox, w_ffn_up, w_ffn_down, ln_g, ln_b):
    slopes = _alibi_slopes()
    yp, ys = x_prompt, x_sample
    bp, sp, _ = yp.shape
    bs, ts, _ = ys.shape
    kmp, vmp, kms, vms = [], [], [], []
    kfp, vfp, lfp, kfs, vfs, lfs = [], [], [], [], [], []
    for i in range(DEPTH):
        li = i // 2
        yp = _post_norm(yp, 0.5 * _swiglu(yp, w_ffn_up[i, 0], w_ffn_down[i, 0]), ln_g[i, 0], ln_b[i, 0])
        ys = _post_norm(ys, 0.5 * _swiglu(ys, w_ffn_up[i, 0], w_ffn_down[i, 0]), ln_g[i, 0], ln_b[i, 0])
        if i % 2 == 0:
            qp, kp, vp = _moba_proj(yp, w_in_moba[li])
            qs, k_s, v_s = _moba_proj(ys, w_in_moba[li])
            mp = _moba_prompt(qp, kp, vp, slopes)
            ms = _moba_sample(qs, k_s, v_s, cache_k_moba, cache_v_moba, li, page_table, slopes)
            w_out = w_out_moba[li]
            kmp.append(kp)
            vmp.append(vp)
            kms.append(k_s)
            vms.append(v_s)
        else:
            qp, kp, vp, lp = _fox_proj(yp, w_in_fox[li], b_f_fox[li])
            qs, k_s, v_s, l_s = _fox_proj(ys, w_in_fox[li], b_f_fox[li])
            mp = _fox_prompt(qp, kp, vp, lp)
            ms = _fox_sample(qs, k_s, v_s, l_s, cache_k_fox, cache_v_fox, cache_logf_fox, li, page_table)
            w_out = w_out_fox[li]
            kfp.append(kp)
            vfp.append(vp)
            lfp.append(lp.astype(cache_logf_fox.dtype))
            kfs.append(k_s)
            vfs.append(v_s)
            lfs.append(l_s.astype(cache_logf_fox.dtype))
        yp = _post_norm(yp, mp.reshape(bp, sp, D_MODEL) @ w_out, ln_g[i, 1], ln_b[i, 1])
        ys = _post_norm(ys, ms.reshape(bs, ts, D_MODEL) @ w_out, ln_g[i, 1], ln_b[i, 1])
        yp = _post_norm(yp, 0.5 * _swiglu(yp, w_ffn_up[i, 1], w_ffn_down[i, 1]), ln_g[i, 2], ln_b[i, 2])
        ys = _post_norm(ys, 0.5 * _swiglu(ys, w_ffn_up[i, 1], w_ffn_down[i, 1]), ln_g[i, 2], ln_b[i, 2])
    return (yp, ys,
            jnp.stack(kmp), jnp.stack(vmp), jnp.stack(kfp), jnp.stack(vfp), jnp.stack(lfp),
            jnp.stack(kms), jnp.stack(vms), jnp.stack(kfs), jnp.stack(vfs), jnp.stack(lfs))
```

```python
import functools

import numpy as np
import jax
import jax.numpy as jnp
from jax import lax
from jax.experimental import pallas as pl
from jax.experimental.pallas import tpu as pltpu

F32 = jnp.float32
BF16 = jnp.bfloat16

D_MODEL = 1024
N_HEADS = 16
HEAD_DIM = D_MODEL // N_HEADS
D_FF = 2816
DEPTH = 2
MOBA_BLOCK = 256
MOBA_TOPK = 3
PAGE_SIZE = 128
DN_ALPHA = (2.0 * DEPTH) ** 0.25
LN_EPS = 1e-5
NEG = -1e30
QK_SCALE = HEAD_DIM ** -0.5

LANES = 128
PAIR = 2 * HEAD_DIM
N_PAIRS = N_HEADS // 2
FF_CHUNK = 1408
PAGES_PER_STEP = 4
VMEM_LIMIT = 56 * 1024 * 1024

_NT = (((1,), (1,)), ((), ()))


def _dot(a, b):
    return jnp.dot(a, b, preferred_element_type=F32)


def _dot_nt(a, b):
    return lax.dot_general(a, b, _NT, preferred_element_type=F32)


def _split3(x):
    hi = x.astype(BF16)
    r1 = x - hi.astype(F32)
    mid = r1.astype(BF16)
    lo = (r1 - mid.astype(F32)).astype(BF16)
    return hi, mid, lo


def _dot_exact_lhs(a_bf, x):
    hi, mid, lo = _split3(x)
    return _dot(a_bf, hi) + _dot(a_bf, mid) + _dot(a_bf, lo)


def _layer_norm(z, g, b):
    mu = jnp.mean(z, axis=-1, keepdims=True)
    zc = z - mu
    var = jnp.mean(zc * zc, axis=-1, keepdims=True)
    return zc * lax.rsqrt(var + LN_EPS) * g + b


def _log_sigmoid(x):
    return jnp.minimum(x, 0.0) - jnp.log1p(jnp.exp(-jnp.abs(x)))


def _params(*sem):
    return pltpu.CompilerParams(dimension_semantics=sem, vmem_limit_bytes=VMEM_LIMIT)


def _ffn_body(x_ref, wg_ref, wu_ref, wd_ref, g_ref, b_ref, o_ref, acc_ref, *, n_chunks):
    c = pl.program_id(1)
    x = x_ref[...]
    xb = x.astype(BF16)
    gate = _dot(xb, wg_ref[...])
    up = _dot(xb, wu_ref[...])
    h = (gate * jax.nn.sigmoid(gate)) * up
    part = _dot(h.astype(BF16), wd_ref[...])

    @pl.when(c == 0)
    def _():
        acc_ref[...] = part

    @pl.when(c > 0)
    def _():
        acc_ref[...] += part

    @pl.when(c == n_chunks - 1)
    def _():
        z = DN_ALPHA * x + 0.5 * acc_ref[...]
        o_ref[...] = _layer_norm(z, g_ref[...], b_ref[...])


def _ffn(x, w_up, w_down, g, b, tm):
    n = x.shape[0]
    nc = D_FF // FF_CHUNK
    return pl.pallas_call(
        functools.partial(_ffn_body, n_chunks=nc),
        grid=(n // tm, nc),
        in_specs=[
            pl.BlockSpec((tm, D_MODEL), lambda i, c: (i, 0)),
            pl.BlockSpec((D_MODEL, FF_CHUNK), lambda i, c: (0, c)),
            pl.BlockSpec((D_MODEL, FF_CHUNK), lambda i, c: (0, c + D_FF // FF_CHUNK)),
            pl.BlockSpec((FF_CHUNK, D_MODEL), lambda i, c: (c, 0)),
            pl.BlockSpec((1, D_MODEL), lambda i, c: (0, 0)),
            pl.BlockSpec((1, D_MODEL), lambda i, c: (0, 0)),
        ],
        out_specs=pl.BlockSpec((tm, D_MODEL), lambda i, c: (i, 0)),
        out_shape=jax.ShapeDtypeStruct((n, D_MODEL), F32),
        scratch_shapes=[pltpu.VMEM((tm, D_MODEL), F32)],
        compiler_params=_params("parallel", "arbitrary"),
        name="ffn",
    )(x, w_up, w_up, w_down, g, b)


def _outproj_t_body(y_ref, mt_ref, w_ref, g_ref, b_ref, o_ref, *, nb):
    for i in range(nb):
        rows = slice(i * MOBA_BLOCK, (i + 1) * MOBA_BLOCK)
        sub = lax.dot_general(mt_ref[i], w_ref[...], (((0,), (0,)), ((), ())),
                              preferred_element_type=F32)
        z = DN_ALPHA * y_ref[rows, :] + sub
        o_ref[rows, :] = _layer_norm(z, g_ref[...], b_ref[...])


def _outproj_t(y, mt, w_out, g, b, tm):
    n = y.shape[0]
    nb = tm // MOBA_BLOCK
    return pl.pallas_call(
        functools.partial(_outproj_t_body, nb=nb),
        grid=(n // tm,),
        in_specs=[
            pl.BlockSpec((tm, D_MODEL), lambda i: (i, 0)),
            pl.BlockSpec((nb, D_MODEL, MOBA_BLOCK), lambda i: (i, 0, 0)),
            pl.BlockSpec((D_MODEL, D_MODEL), lambda i: (0, 0)),
            pl.BlockSpec((1, D_MODEL), lambda i: (0, 0)),
            pl.BlockSpec((1, D_MODEL), lambda i: (0, 0)),
        ],
        out_specs=pl.BlockSpec((tm, D_MODEL), lambda i: (i, 0)),
        out_shape=jax.ShapeDtypeStruct((n, D_MODEL), F32),
        compiler_params=_params("parallel"),
        name="outproj_t",
    )(y, mt, w_out, g, b)


def _outproj_body(y_ref, m_ref, w_ref, g_ref, b_ref, o_ref):
    sub = _dot(m_ref[...].astype(BF16), w_ref[...])
    z = DN_ALPHA * y_ref[...] + sub
    o_ref[...] = _layer_norm(z, g_ref[...], b_ref[...])


def _outproj(y, m, w_out, g, b):
    n = y.shape[0]
    full = lambda shape: pl.BlockSpec(shape, lambda i: (0, 0))
    return pl.pallas_call(
        _outproj_body,
        grid=(1,),
        in_specs=[full((n, D_MODEL)), full((n, D_MODEL)), full((D_MODEL, D_MODEL)),
                  full((1, D_MODEL)), full((1, D_MODEL))],
        out_specs=full((n, D_MODEL)),
        out_shape=jax.ShapeDtypeStruct((n, D_MODEL), F32),
        compiler_params=_params("arbitrary"),
        name="outproj",
    )(y, m, w_out, g, b)


def _proj_common(y_ref, wk_ref, wv_ref, wqt_ref, wvt_ref, k_ref, v_ref, kb_ref, qt_ref, vt_ref, nb):
    yb = y_ref[...].astype(BF16)
    k = _dot(yb, wk_ref[...])
    k_ref[...] = k
    v_ref[...] = _dot(yb, wv_ref[...])
    kb = k.astype(BF16)
    qt = _dot_nt(wqt_ref[...], yb)
    vt = _dot_nt(wvt_ref[...], yb)
    for i in range(nb):
        cols = slice(i * MOBA_BLOCK, (i + 1) * MOBA_BLOCK)
        kb_ref[i] = kb[cols, :]
        qt_ref[i] = (qt[:, cols] * QK_SCALE).astype(BF16)
        vt_ref[i] = vt[:, cols].astype(BF16)
    return k, qt


def _proj_moba_body(y_ref, wk_ref, wv_ref, wqt_ref, wvt_ref, hm_ref,
                    k_ref, v_ref, kb_ref, qt_ref, vt_ref, sel_ref, km_ref, *, nb, n_blk):
    t = pl.program_id(1)

    @pl.when(t == 0)
    def _():
        km_ref[...] = jnp.zeros_like(km_ref)

    k, qt = _proj_common(y_ref, wk_ref, wv_ref, wqt_ref, wvt_ref,
                         k_ref, v_ref, kb_ref, qt_ref, vt_ref, nb)
    for i in range(nb):
        rows = slice(i * MOBA_BLOCK, (i + 1) * MOBA_BLOCK)
        km_ref[pl.ds(t * nb + i, 1), :] = jnp.sum(k[rows, :], axis=0, keepdims=True) / MOBA_BLOCK

    km = km_ref[...]
    hm = hm_ref[...]
    kmb = jnp.concatenate(
        [jnp.broadcast_to(km[n:n + 1, :], (N_HEADS, D_MODEL)) * hm for n in range(n_blk)], axis=0)
    a_hi = kmb.astype(BF16)
    a_mid = (kmb - a_hi.astype(F32)).astype(BF16)
    q_hi = qt.astype(BF16)
    q_mid = (qt - q_hi.astype(F32)).astype(BF16)
    gate = _dot(a_hi, q_hi) + _dot(a_hi, q_mid) + _dot(a_mid, q_hi)

    tm = nb * MOBA_BLOCK
    col = lax.broadcasted_iota(jnp.int32, (N_HEADS, tm), 1)
    q_blk = t * nb + col // MOBA_BLOCK
    g = []
    for n in range(n_blk):
        g.append(jnp.where(n < q_blk, gate[n * N_HEADS:(n + 1) * N_HEADS, :], NEG))
    bias = []
    for n in range(n_blk):
        rank = jnp.zeros((N_HEADS, tm), jnp.int32)
        for m in range(n_blk):
            if m == n:
                continue
            ahead = (g[m] >= g[n]) if m < n else (g[m] > g[n])
            rank = rank + ahead.astype(jnp.int32)
        keep = (n < q_blk) & (rank < MOBA_TOPK)
        bias.append(jnp.where(keep, 0.0, NEG))
    bias = jnp.concatenate(bias, axis=0)
    for i in range(nb):
        sel_ref[i] = bias[:, i * MOBA_BLOCK:(i + 1) * MOBA_BLOCK]


def _proj_moba(y, wk, wv, wqt, wvt, hm, batch, seq, tm):
    n = y.shape[0]
    nb = tm // MOBA_BLOCK
    n_blk = seq // MOBA_BLOCK
    tiles = seq // tm
    full = lambda shape: pl.BlockSpec(shape, lambda b, t: (0,) * len(shape))
    row = lambda b, t: (b * tiles + t, 0)
    blk = lambda b, t: (b * tiles + t, 0, 0)
    nblocks = n // MOBA_BLOCK
    return pl.pallas_call(
        functools.partial(_proj_moba_body, nb=nb, n_blk=n_blk),
        grid=(batch, tiles),
        in_specs=[pl.BlockSpec((tm, D_MODEL), row), full((D_MODEL, D_MODEL)), full((D_MODEL, D_MODEL)),
                  full((D_MODEL, D_MODEL)), full((D_MODEL, D_MODEL)), full((N_HEADS, D_MODEL))],
        out_specs=[
            pl.BlockSpec((tm, D_MODEL), row),
            pl.BlockSpec((tm, D_MODEL), row),
            pl.BlockSpec((nb, MOBA_BLOCK, D_MODEL), blk),
            pl.BlockSpec((nb, D_MODEL, MOBA_BLOCK), blk),
            pl.BlockSpec((nb, D_MODEL, MOBA_BLOCK), blk),
            pl.BlockSpec((nb, n_blk * N_HEADS, MOBA_BLOCK), blk),
        ],
        out_shape=[
            jax.ShapeDtypeStruct((n, D_MODEL), F32),
            jax.ShapeDtypeStruct((n, D_MODEL), F32),
            jax.ShapeDtypeStruct((nblocks, MOBA_BLOCK, D_MODEL), BF16),
            jax.ShapeDtypeStruct((nblocks, D_MODEL, MOBA_BLOCK), BF16),
            jax.ShapeDtypeStruct((nblocks, D_MODEL, MOBA_BLOCK), BF16),
            jax.ShapeDtypeStruct((nblocks, n_blk * N_HEADS, MOBA_BLOCK), F32),
        ],
        scratch_shapes=[pltpu.VMEM((n_blk, D_MODEL), F32)],
        compiler_params=_params("arbitrary", "arbitrary"),
        name="proj_moba",
    )(y, wk, wv, wqt, wvt, hm)


def _logf_from(yb, wf_ref, bf_ref, tm):
    lane = lax.broadcasted_iota(jnp.int32, (tm, LANES), 1)
    zf = _dot(yb, wf_ref[...]) + bf_ref[...]
    return jnp.where(lane < N_HEADS, _log_sigmoid(zf), 0.0)


def _pack3(x):
    hi, mid, lo = _split3(x)
    packed = (hi.astype(F32) + pltpu.roll(mid.astype(F32), N_HEADS, 1)
              + pltpu.roll(lo.astype(F32), 2 * N_HEADS, 1))
    return packed.astype(BF16)


def _proj_fox_body(y_ref, wk_ref, wv_ref, wqt_ref, wvt_ref, wf_ref, bf_ref, tri_ref, place_ref,
                   k_ref, v_ref, kb_ref, qt_ref, vt_ref, lf_ref, ck_ref, carry_ref, *, nb):
    t = pl.program_id(1)

    @pl.when(t == 0)
    def _():
        carry_ref[...] = jnp.zeros_like(carry_ref)

    _proj_common(y_ref, wk_ref, wv_ref, wqt_ref, wvt_ref, k_ref, v_ref, kb_ref, qt_ref, vt_ref, nb)
    tm = nb * MOBA_BLOCK
    yb = y_ref[...].astype(BF16)
    lf = _logf_from(yb, wf_ref, bf_ref, tm)
    lf_ref[...] = lf[:, :N_HEADS]
    ct = _dot_exact_lhs(tri_ref[...], lf) + carry_ref[0:1, :]
    carry_ref[0:1, :] = ct[tm - 1:tm, :]
    ck = _dot(_pack3(ct), place_ref[...]).astype(BF16)
    for i in range(nb):
        ck_ref[i] = ck[i * MOBA_BLOCK:(i + 1) * MOBA_BLOCK, :]


def _proj_fox(y, wk, wv, wqt, wvt, wf, bf, tri, place, batch, seq, tm):
    n = y.shape[0]
    nb = tm // MOBA_BLOCK
    tiles = seq // tm
    full = lambda shape: pl.BlockSpec(shape, lambda b, t: (0,) * len(shape))
    row = lambda b, t: (b * tiles + t, 0)
    blk = lambda b, t: (b * tiles + t, 0, 0)
    nblocks = n // MOBA_BLOCK
    return pl.pallas_call(
        functools.partial(_proj_fox_body, nb=nb),
        grid=(batch, tiles),
        in_specs=[pl.BlockSpec((tm, D_MODEL), row), full((D_MODEL, D_MODEL)), full((D_MODEL, D_MODEL)),
                  full((D_MODEL, D_MODEL)), full((D_MODEL, D_MODEL)), full((D_MODEL, LANES)),
                  full((1, LANES)), full((tm, tm)), full((LANES, D_MODEL))],
        out_specs=[
            pl.BlockSpec((tm, D_MODEL), row),
            pl.BlockSpec((tm, D_MODEL), row),
            pl.BlockSpec((nb, MOBA_BLOCK, D_MODEL), blk),
            pl.BlockSpec((nb, D_MODEL, MOBA_BLOCK), blk),
            pl.BlockSpec((nb, D_MODEL, MOBA_BLOCK), blk),
            pl.BlockSpec((tm, N_HEADS), row),
            pl.BlockSpec((nb, MOBA_BLOCK, D_MODEL), blk),
        ],
        out_shape=[
            jax.ShapeDtypeStruct((n, D_MODEL), F32),
            jax.ShapeDtypeStruct((n, D_MODEL), F32),
            jax.ShapeDtypeStruct((nblocks, MOBA_BLOCK, D_MODEL), BF16),
            jax.ShapeDtypeStruct((nblocks, D_MODEL, MOBA_BLOCK), BF16),
            jax.ShapeDtypeStruct((nblocks, D_MODEL, MOBA_BLOCK), BF16),
            jax.ShapeDtypeStruct((n, N_HEADS), F32),
            jax.ShapeDtypeStruct((nblocks, MOBA_BLOCK, D_MODEL), BF16),
        ],
        scratch_shapes=[pltpu.VMEM((8, LANES), F32)],
        compiler_params=_params("arbitrary", "arbitrary"),
        name="proj_fox",
    )(y, wk, wv, wqt, wvt, wf, bf, tri, place)


def _proj_nat_body(y_ref, wq_ref, wk_ref, wv_ref, *rest, with_logf):
    if with_logf:
        wf_ref, bf_ref, q_ref, k_ref, v_ref, lf_ref = rest
    else:
        q_ref, k_ref, v_ref = rest
    yb = y_ref[...].astype(BF16)
    q_ref[...] = _dot(yb, wq_ref[...])
    k_ref[...] = _dot(yb, wk_ref[...])
    v_ref[...] = _dot(yb, wv_ref[...])
    if with_logf:
        lf_ref[...] = _logf_from(yb, wf_ref, bf_ref, y_ref.shape[0])[:, :N_HEADS]


def _proj_nat(y, wq, wk, wv, wf=None, bf=None):
    n = y.shape[0]
    with_logf = wf is not None
    full = lambda shape: pl.BlockSpec(shape, lambda i: (0,) * len(shape))
    ins = [y, wq, wk, wv]
    in_specs = [full((n, D_MODEL))] + [full((D_MODEL, D_MODEL))] * 3
    out_specs = [full((n, D_MODEL))] * 3
    out_shape = [jax.ShapeDtypeStruct((n, D_MODEL), F32)] * 3
    if with_logf:
        ins += [wf, bf]
        in_specs += [full((D_MODEL, LANES)), full((1, LANES))]
        out_specs += [full((n, N_HEADS))]
        out_shape += [jax.ShapeDtypeStruct((n, N_HEADS), F32)]
    return pl.pallas_call(
        functools.partial(_proj_nat_body, with_logf=with_logf),
        grid=(1,),
        in_specs=in_specs,
        out_specs=out_specs,
        out_shape=out_shape,
        compiler_params=_params("arbitrary"),
        name="proj_nat",
    )(*ins)


def _attn_body(kb_ref, aux_ref, qt_ref, vt_ref, qa_ref, *rest, n_blk, use_sel):
    if use_sel:
        sel_ref, o_ref = rest
    else:
        (o_ref,) = rest
    hp = pl.program_id(1)
    key_i = lax.broadcasted_iota(jnp.int32, (MOBA_BLOCK, MOBA_BLOCK), 0)
    qry_i = lax.broadcasted_iota(jnp.int32, (MOBA_BLOCK, MOBA_BLOCK), 1)
    causal = key_i <= qry_i
    rid = lax.broadcasted_iota(jnp.int32, (PAIR, MOBA_BLOCK), 0)

    for i in range(2):
        hrows = slice(i * HEAD_DIM, (i + 1) * HEAD_DIM)
        own_rows = (rid >= i * HEAD_DIM) & (rid < (i + 1) * HEAD_DIM)
        qa = qa_ref[i]

        def q_tile(j, carry, i=i, hrows=hrows, own_rows=own_rows, qa=qa):
            qt = qt_ref[j]
            qaug = jnp.concatenate([jnp.where(own_rows, qt, jnp.zeros_like(qt)), qa], axis=0)

            def scores(n):
                kcat = jnp.concatenate([kb_ref[n], aux_ref[n]], axis=1)
                return _dot(kcat, qaug)

            s = jnp.where(causal, scores(j), NEG)
            m = jnp.max(s, axis=0, keepdims=True)
            p = jnp.exp(s - m)
            l = jnp.sum(p, axis=0, keepdims=True)
            acc = _dot(vt_ref[j, hrows, :], p.astype(BF16))

            def past(n, c):
                m, l, acc = c
                s = scores(n)
                if use_sel:
                    s = s + sel_ref[j, pl.ds(n * N_HEADS + 2 * hp + i, 1), :]
                m_new = jnp.maximum(m, jnp.max(s, axis=0, keepdims=True))
                corr = jnp.exp(m - m_new)
                p = jnp.exp(s - m_new)
                l = l * corr + jnp.sum(p, axis=0, keepdims=True)
                acc = acc * corr + _dot(vt_ref[n, hrows, :], p.astype(BF16))
                return m_new, l, acc

            m, l, acc = lax.fori_loop(0, j, past, (m, l, acc))
            o_ref[j, hrows, :] = (acc / l).astype(BF16)
            return carry

        lax.fori_loop(0, n_blk, q_tile, 0)


def _attn(kb, aux, qt, vt, qa, sel, batch, seq, aux_is_const):
    n_blk = seq // MOBA_BLOCK
    nblocks = kb.shape[0]
    use_sel = sel is not None
    aux_map = (lambda b, hp: (0, 0, 0)) if aux_is_const else (lambda b, hp: (b, 0, hp))
    in_specs = [
        pl.BlockSpec((n_blk, MOBA_BLOCK, PAIR), lambda b, hp: (b, 0, hp)),
        pl.BlockSpec((n_blk, MOBA_BLOCK, PAIR), aux_map),
        pl.BlockSpec((n_blk, PAIR, MOBA_BLOCK), lambda b, hp: (b, hp, 0)),
        pl.BlockSpec((n_blk, PAIR, MOBA_BLOCK), lambda b, hp: (b, hp, 0)),
        pl.BlockSpec((2, PAIR, MOBA_BLOCK), lambda b, hp: (hp, 0, 0)),
    ]
    ins = [kb, aux, qt, vt, qa]
    if use_sel:
        in_specs.append(pl.BlockSpec((n_blk, n_blk * N_HEADS, MOBA_BLOCK), lambda b, hp: (b, 0, 0)))
        ins.append(sel)
    return pl.pallas_call(
        functools.partial(_attn_body, n_blk=n_blk, use_sel=use_sel),
        grid=(batch, N_PAIRS),
        in_specs=in_specs,
        out_specs=pl.BlockSpec((n_blk, PAIR, MOBA_BLOCK), lambda b, hp: (b, hp, 0)),
        out_shape=jax.ShapeDtypeStruct((nblocks, D_MODEL, MOBA_BLOCK), BF16),
        compiler_params=_params("parallel", "parallel"),
        name="attn_moba" if use_sel else "attn_fox",
    )(*ins)


def _diag_rows(x, t):
    return jnp.sum(x.reshape(N_HEADS, t, D_MODEL), axis=0)


def _expand(c, dm, t):
    return _diag_rows(jnp.broadcast_to(c, dm.shape) * dm, t)


def _moba_sample_body(pt_ref, q_ref, kn_ref, vn_ref, dm_ref, slb_ref, qp_ref, nbias_ref, *rest,
                      pp, t_new, n_blk):
    del pt_ref
    k_refs, v_refs = rest[:pp], rest[pp:2 * pp]
    o_ref, qbt_ref, km_ref, m_ref, l_ref, od_ref = rest[2 * pp:]
    s = pl.program_id(1)
    rows = N_HEADS * t_new
    dm = dm_ref[...]
    lane = lax.broadcasted_iota(jnp.int32, (rows, LANES), 1)

    @pl.when(s == 0)
    def _():
        q16 = jnp.concatenate([q_ref[...]] * N_HEADS, axis=0) * dm
        qbt_ref[...] = (q16 * QK_SCALE).astype(BF16)
        km_ref[...] = jnp.zeros_like(km_ref)
        m_ref[...] = jnp.zeros_like(m_ref)
        l_ref[...] = jnp.zeros_like(l_ref)

    qbt = qbt_ref[...]
    per_blk = MOBA_BLOCK // PAGE_SIZE
    for i in range(pp // per_blk):
        n = s * (pp // per_blk) + i
        kpg = jnp.concatenate([k_refs[per_blk * i + j][...] for j in range(per_blk)], axis=0)
        vpg = jnp.concatenate([v_refs[per_blk * i + j][...] for j in range(per_blk)], axis=0)
        km_ref[pl.ds(n, 1), :] = jnp.sum(kpg, axis=0, keepdims=True) / MOBA_BLOCK
        sc = _dot_nt(qbt, kpg.astype(BF16))
        dist = qp_ref[...] - (n * MOBA_BLOCK).astype(F32)
        sc = sc - slb_ref[...] * dist
        m = jnp.max(sc, axis=1, keepdims=True)
        p = jnp.exp(sc - m)
        l = jnp.sum(p, axis=1, keepdims=True)
        o = _dot(p.astype(BF16), vpg.astype(BF16))
        od_ref[n] = _diag_rows(o * dm, t_new)
        onehot = (lane == n).astype(F32)
        m_ref[...] += m * onehot
        l_ref[...] += l * onehot

    @pl.when(s == pl.num_programs(1) - 1)
    def _():
        q16 = jnp.concatenate([q_ref[...]] * N_HEADS, axis=0) * dm
        km = km_ref[...]
        a_hi = q16.astype(BF16)
        a_mid = (q16 - a_hi.astype(F32)).astype(BF16)
        b_hi = km.astype(BF16)
        b_mid = (km - b_hi.astype(F32)).astype(BF16)
        gate = _dot_nt(a_hi, b_hi) + _dot_nt(a_hi, b_mid) + _dot_nt(a_mid, b_hi)
        low = jnp.float32(-3.0e38)
        g = jnp.where(lane < n_blk, gate, low)
        sel = lane < 0
        for _ in range(MOBA_TOPK):
            mx = jnp.max(g, axis=1, keepdims=True)
            first = jnp.min(jnp.where(g == mx, lane, LANES), axis=1, keepdims=True)
            pick = lane == first
            sel = sel | pick
            g = jnp.where(pick, low, g)
        pad = jnp.zeros((LANES - t_new, D_MODEL), F32)
        kn = jnp.concatenate([kn_ref[...], pad], axis=0).astype(BF16)
        vn = jnp.concatenate([vn_ref[...], pad], axis=0).astype(BF16)
        sn = _dot_nt(qbt, kn) + nbias_ref[...]
        m_n = jnp.max(sn, axis=1, keepdims=True)
        p_n = jnp.exp(sn - m_n)
        l_n = jnp.sum(p_n, axis=1, keepdims=True)
        od_n = _diag_rows(_dot(p_n.astype(BF16), vn) * dm, t_new)
        m_all = m_ref[...]
        m_tot = jnp.maximum(jnp.max(jnp.where(sel, m_all, NEG), axis=1, keepdims=True), m_n)
        w = jnp.where(sel, jnp.exp(jnp.minimum(m_all - m_tot, 0.0)), 0.0)
        w_n = jnp.exp(m_n - m_tot)
        l_tot = jnp.sum(w * l_ref[...], axis=1, keepdims=True) + w_n * l_n
        acc = _expand(w_n, dm, t_new) * od_n
        for n in range(n_blk):
            acc = acc + _expand(w[:, n:n + 1], dm, t_new) * od_ref[n]
        o_ref[...] = acc / _expand(l_tot, dm, t_new)


def _page_spec(i, pp, n_pages, reverse):
    def index(b, s, pt):
        p = s * pp + i
        if reverse:
            p = n_pages - 1 - p
        return (pt[b * n_pages + p], 0, 0)
    return index


def _moba_sample(q, k_new, v_new, cache_k, cache_v, page_table, consts, t_new):
    batch = q.shape[0] // t_new
    n_pages = page_table.shape[1]
    n_blk = n_pages * PAGE_SIZE // MOBA_BLOCK
    pp = PAGES_PER_STEP
    rows = N_HEADS * t_new
    dm, slb, qp, nbias = consts
    tok = pl.BlockSpec((t_new, D_MODEL), lambda b, s, pt: (b, 0))
    const = lambda shape: pl.BlockSpec(shape, lambda b, s, pt: (0, 0))
    page = lambda i: pl.BlockSpec((None, PAGE_SIZE, D_MODEL), _page_spec(i, pp, n_pages, False))
    grid_spec = pltpu.PrefetchScalarGridSpec(
        num_scalar_prefetch=1,
        grid=(batch, n_pages // pp),
        in_specs=[tok, tok, tok, const((rows, D_MODEL)), const((rows, MOBA_BLOCK)),
                  const((rows, MOBA_BLOCK)), const((rows, LANES))]
        + [page(i) for i in range(pp)] + [page(i) for i in range(pp)],
        out_specs=tok,
        scratch_shapes=[
            pltpu.VMEM((rows, D_MODEL), BF16),
            pltpu.VMEM((LANES, D_MODEL), F32),
            pltpu.VMEM((rows, LANES), F32),
            pltpu.VMEM((rows, LANES), F32),
            pltpu.VMEM((n_blk, t_new, D_MODEL), F32),
        ],
    )
    return pl.pallas_call(
        functools.partial(_moba_sample_body, pp=pp, t_new=t_new, n_blk=n_blk),
        grid_spec=grid_spec,
        out_shape=jax.ShapeDtypeStruct((batch * t_new, D_MODEL), F32),
        compiler_params=_params("parallel", "arbitrary"),
        name="moba_sample",
    )(page_table.reshape(-1), q, k_new, v_new, dm, slb, qp, nbias,
      *([cache_k] * pp), *([cache_v] * pp))


def _fox_sample_body(pt_ref, q_ref, kn_ref, vn_ref, lfn_ref, dm_ref, qaux_ref, wst_ref, tri_ref,
                     nbias_ref, *rest, pp, t_new):
    del pt_ref
    k_refs, v_refs, lf_refs = rest[:pp], rest[pp:2 * pp], rest[2 * pp:3 * pp]
    o_ref, qbt_ref, pad_ref, carry_ref, m_ref, l_ref, acc_ref = rest[3 * pp:]
    s = pl.program_id(1)
    dm = dm_ref[...]

    @pl.when(s == 0)
    def _():
        q16 = jnp.concatenate([q_ref[...]] * N_HEADS, axis=0) * dm
        qbt_ref[:, :D_MODEL] = (q16 * QK_SCALE).astype(BF16)
        qbt_ref[:, D_MODEL:] = qaux_ref[...]
        pad_ref[...] = jnp.zeros_like(pad_ref)
        carry_ref[...] = jnp.zeros_like(carry_ref)
        m_ref[...] = jnp.full_like(m_ref, NEG)
        l_ref[...] = jnp.zeros_like(l_ref)
        acc_ref[...] = jnp.zeros_like(acc_ref)

    qbt = qbt_ref[...]

    def online(sc, vals):
        m_old = m_ref[:, 0:1]
        m_new = jnp.maximum(m_old, jnp.max(sc, axis=1, keepdims=True))
        corr = jnp.exp(m_old - m_new)
        p = jnp.exp(sc - m_new)
        l_ref[...] = jnp.broadcast_to(l_ref[:, 0:1] * corr + jnp.sum(p, axis=1, keepdims=True), l_ref.shape)
        acc_ref[...] = acc_ref[...] * corr + _dot(p.astype(BF16), vals)
        m_ref[...] = jnp.broadcast_to(m_new, m_ref.shape)

    for i in range(pp):
        pad_ref[:, :N_HEADS] = lf_refs[i][...]
        lf = pad_ref[...]
        suf = _dot_exact_lhs(wst_ref[...], lf) + carry_ref[0:1, :]
        carry_ref[0:1, :] = carry_ref[0:1, :] + jnp.sum(lf, axis=0, keepdims=True)
        kcat = jnp.concatenate([k_refs[i][...].astype(BF16), _pack3(suf)], axis=1)
        online(_dot_nt(qbt, kcat), v_refs[i][...].astype(BF16))

    @pl.when(s == pl.num_programs(1) - 1)
    def _():
        pad = jnp.zeros((LANES - t_new, D_MODEL), F32)
        pad_ref[...] = jnp.zeros_like(pad_ref)
        pad_ref[0:t_new, :N_HEADS] = lfn_ref[...]
        c_new = _dot_exact_lhs(tri_ref[...], pad_ref[...])
        kn = jnp.concatenate([kn_ref[...], pad], axis=0).astype(BF16)
        vn = jnp.concatenate([vn_ref[...], pad], axis=0).astype(BF16)
        kcat = jnp.concatenate([kn, _pack3(-c_new)], axis=1)
        online(_dot_nt(qbt, kcat) + nbias_ref[...], vn)
        out = acc_ref[...] / l_ref[:, 0:1]
        o_ref[...] = _diag_rows(out * dm, t_new)


def _fox_sample(q, k_new, v_new, lf_new, cache_k, cache_v, cache_lf, page_table, consts, t_new):
    batch = q.shape[0] // t_new
    n_pages = page_table.shape[1]
    pp = PAGES_PER_STEP
    rows = N_HEADS * t_new
    dm, qaux, wst, tri, nbias = consts
    tok = pl.BlockSpec((t_new, D_MODEL), lambda b, s, pt: (b, 0))
    const = lambda shape: pl.BlockSpec(shape, lambda b, s, pt: (0, 0))
    page = lambda i: pl.BlockSpec((None, PAGE_SIZE, D_MODEL), _page_spec(i, pp, n_pages, True))
    lfpage = lambda i: pl.BlockSpec((None, PAGE_SIZE, N_HEADS), _page_spec(i, pp, n_pages, True))
    grid_spec = pltpu.PrefetchScalarGridSpec(
        num_scalar_prefetch=1,
        grid=(batch, n_pages // pp),
        in_specs=[tok, tok, tok, pl.BlockSpec((t_new, N_HEADS), lambda b, s, pt: (b, 0)),
                  const((rows, D_MODEL)), const((rows, LANES)), const((PAGE_SIZE, PAGE_SIZE)),
                  const((LANES, LANES)), const((rows, LANES))]
        + [page(i) for i in range(pp)] + [page(i) for i in range(pp)] + [lfpage(i) for i in range(pp)],
        out_specs=tok,
        scratch_shapes=[
            pltpu.VMEM((rows, D_MODEL + LANES), BF16),
            pltpu.VMEM((PAGE_SIZE, LANES), F32),
            pltpu.VMEM((8, LANES), F32),
            pltpu.VMEM((rows, LANES), F32),
            pltpu.VMEM((rows, LANES), F32),
            pltpu.VMEM((rows, D_MODEL), F32),
        ],
    )
    return pl.pallas_call(
        functools.partial(_fox_sample_body, pp=pp, t_new=t_new),
        grid_spec=grid_spec,
        out_shape=jax.ShapeDtypeStruct((batch * t_new, D_MODEL), F32),
        compiler_params=_params("parallel", "arbitrary"),
        name="fox_sample",
    )(page_table.reshape(-1), q, k_new, v_new, lf_new, dm, qaux, wst, tri, nbias,
      *([cache_k] * pp), *([cache_v] * pp), *([cache_lf] * pp))


def _np_split3(x):
    x = jnp.asarray(x, F32)
    hi, mid, lo = _split3(x)
    return hi, mid, lo


def _alibi_slopes():
    return 2.0 ** (-8.0 * np.arange(1, N_HEADS + 1) / N_HEADS)


def _prompt_tables(seq, tm):
    n_blk = seq // MOBA_BLOCK
    pos = np.zeros((n_blk, MOBA_BLOCK, PAIR), np.float32)
    pos[:, :, 0:3] = np.arange(MOBA_BLOCK, dtype=np.float32)[None, :, None]
    pos[:, :, 3:6] = (np.arange(n_blk, dtype=np.float32) * MOBA_BLOCK)[:, None, None]
    s_hi, s_mid, s_lo = _np_split3(_alibi_slopes())
    parts = jnp.stack([s_hi, s_mid, s_lo, s_hi, s_mid, s_lo], axis=1).astype(F32)
    qa_moba = jnp.zeros((N_HEADS, PAIR, MOBA_BLOCK), F32)
    qa_moba = qa_moba.at[:, 0:6, :].set(jnp.broadcast_to(parts[:, :, None], (N_HEADS, 6, MOBA_BLOCK)))
    qa_fox = np.zeros((N_HEADS, PAIR, MOBA_BLOCK), np.float32)
    for h in range(N_HEADS):
        i = h % 2
        qa_fox[h, 3 * i:3 * i + 3, :] = -1.0
    place = np.zeros((LANES, D_MODEL), np.float32)
    for h in range(N_HEADS):
        for piece in range(3):
            place[piece * N_HEADS + h, (h // 2) * PAIR + 3 * (h % 2) + piece] = 1.0
    tri = np.tril(np.ones((tm, tm), np.float32))
    hm = (np.arange(D_MODEL)[None, :] // HEAD_DIM == np.arange(N_HEADS)[:, None]).astype(np.float32)
    return dict(pos=jnp.asarray(pos, BF16), qa_moba=qa_moba.astype(BF16), qa_fox=jnp.asarray(qa_fox, BF16),
                place=jnp.asarray(place, BF16), tri=jnp.asarray(tri, BF16), hm=jnp.asarray(hm, F32))


def _sample_tables(t_new, past_len):
    rows = N_HEADS * t_new
    h_of = np.arange(rows) // t_new
    t_of = np.arange(rows) % t_new
    dm = (np.arange(D_MODEL)[None, :] // HEAD_DIM == h_of[:, None]).astype(np.float32)
    slopes = _alibi_slopes().astype(np.float32)
    slb = np.broadcast_to(slopes[h_of][:, None], (rows, MOBA_BLOCK)).astype(np.float32)
    qp = (past_len + t_of[:, None] - np.arange(MOBA_BLOCK)[None, :]).astype(np.float32)
    s_idx = np.arange(LANES)[None, :]
    visible = (s_idx <= t_of[:, None]) & (s_idx < t_new)
    nb_moba = np.where(visible, -slopes[h_of][:, None] * (t_of[:, None] - s_idx), NEG).astype(np.float32)
    nb_fox = np.where(visible, 0.0, NEG).astype(np.float32)
    qaux = np.zeros((rows, LANES), np.float32)
    for piece in range(3):
        qaux[np.arange(rows), piece * N_HEADS + h_of] = 1.0
    wst = np.triu(np.ones((PAGE_SIZE, PAGE_SIZE), np.float32), k=1)
    tri = np.tril(np.ones((LANES, LANES), np.float32))
    return dict(dm=jnp.asarray(dm), slb=jnp.asarray(slb), qp=jnp.asarray(qp), nb_moba=jnp.asarray(nb_moba),
                nb_fox=jnp.asarray(nb_fox), qaux=jnp.asarray(qaux, BF16), wst=jnp.asarray(wst, BF16),
                tri=jnp.asarray(tri, BF16))


def kernel(x_prompt, x_sample, cache_k_moba, cache_v_moba, cache_k_fox, cache_v_fox, cache_logf_fox,
           page_table, w_in_moba, w_out_moba, w_in_fox, b_f_fox, w_out_fox, w_ffn_up, w_ffn_down,
           ln_g, ln_b):
    batch, seq, d = x_prompt.shape
    dec_batch, t_new, _ = x_sample.shape
    n_pool = cache_k_moba.shape[1]
    past_len = page_table.shape[1] * PAGE_SIZE
    assert d == D_MODEL and seq % 512 == 0 and t_new == 8
    tm = 512
    pt = _prompt_tables(seq, tm)
    st = _sample_tables(t_new, past_len)

    yp = x_prompt.reshape(batch * seq, d)
    ys = x_sample.reshape(dec_batch * t_new, d)
    n_s = ys.shape[0]

    def norm(i, j):
        return ln_g[i, j].reshape(1, d), ln_b[i, j].reshape(1, d)

    def ffn_both(yp, ys, i, j):
        wu = w_ffn_up[i, j].astype(BF16)
        wd = w_ffn_down[i, j].astype(BF16)
        g, b = norm(i, 2 * j)
        return _ffn(yp, wu, wd, g, b, tm), _ffn(ys, wu, wd, g, b, n_s)

    def heads(z, lead):
        return z.reshape(lead + (N_HEADS, HEAD_DIM))[None]

    outs = {}
    for i in range(DEPTH):
        li = i // 2
        yp, ys = ffn_both(yp, ys, i, 0)
        if i % 2 == 0:
            w = w_in_moba[li].astype(BF16)
            wq, wk, wv = w[:, :d], w[:, d:2 * d], w[:, 2 * d:3 * d]
            kp, vp, kb, qt, vt, sel = _proj_moba(yp, wk, wv, wq.T, wv.T, pt["hm"], batch, seq, tm)
            qs, ks, vs = _proj_nat(ys, wq, wk, wv)
            mt = _attn(kb, pt["pos"], qt, vt, pt["qa_moba"], sel, batch, seq, True)
            ms = _moba_sample(qs, ks, vs, cache_k_moba[li].reshape(n_pool, PAGE_SIZE, d),
                              cache_v_moba[li].reshape(n_pool, PAGE_SIZE, d), page_table,
                              (st["dm"], st["slb"], st["qp"], st["nb_moba"]), t_new)
            w_out = w_out_moba[li].astype(BF16)
            outs["kmp"], outs["vmp"] = heads(kp, (batch, seq)), heads(vp, (batch, seq))
            outs["kms"], outs["vms"] = heads(ks, (dec_batch, t_new)), heads(vs, (dec_batch, t_new))
        else:
            w = w_in_fox[li].astype(BF16)
            wq, wk, wv = w[:, :d], w[:, d:2 * d], w[:, 2 * d:3 * d]
            wf = jnp.pad(w[:, 3 * d:], ((0, 0), (0, LANES - N_HEADS)))
            bf = jnp.pad(b_f_fox[li].astype(F32).reshape(1, N_HEADS), ((0, 0), (0, LANES - N_HEADS)))
            kp, vp, kb, qt, vt, lfp, ck = _proj_fox(yp, wk, wv, wq.T, wv.T, wf, bf, pt["tri"], pt["place"],
                                                   batch, seq, tm)
            qs, ks, vs, lfs = _proj_nat(ys, wq, wk, wv, wf, bf)
            mt = _attn(kb, ck, qt, vt, pt["qa_fox"], None, batch, seq, False)
            ms = _fox_sample(qs, ks, vs, lfs, cache_k_fox[li].reshape(n_pool, PAGE_SIZE, d),
                             cache_v_fox[li].reshape(n_pool, PAGE_SIZE, d), cache_logf_fox[li], page_table,
                             (st["dm"], st["qaux"], st["wst"], st["tri"], st["nb_fox"]), t_new)
            w_out = w_out_fox[li].astype(BF16)
            outs["kfp"], outs["vfp"] = heads(kp, (batch, seq)), heads(vp, (batch, seq))
            outs["lfp"] = lfp.reshape(batch, seq, N_HEADS)[None]
            outs["kfs"], outs["vfs"] = heads(ks, (dec_batch, t_new)), heads(vs, (dec_batch, t_new))
            outs["lfs"] = lfs.reshape(dec_batch, t_new, N_HEADS)[None]
        g, b = norm(i, 1)
        yp = _outproj_t(yp, mt, w_out, g, b, tm)
        ys = _outproj(ys, ms, w_out, g, b)
        yp, ys = ffn_both(yp, ys, i, 1)

    return (yp.reshape(batch, seq, d), ys.reshape(dec_batch, t_new, d),
            outs["kmp"], outs["vmp"], outs["kfp"], outs["vfp"], outs["lfp"],
            outs["kms"], outs["vms"], outs["kfs"], outs["vfs"], outs["lfs"])
```

```python
import functools

import numpy as np
import jax
import jax.numpy as jnp
from jax import lax
from jax.experimental import pallas as pl
from jax.experimental.pallas import tpu as pltpu

F32 = jnp.float32
BF16 = jnp.bfloat16

D_MODEL = 1024
N_HEADS = 16
HEAD_DIM = D_MODEL // N_HEADS
D_FF = 2816
DEPTH = 2
MOBA_BLOCK = 256
MOBA_TOPK = 3
PAGE_SIZE = 128
DN_ALPHA = (2.0 * DEPTH) ** 0.25
LN_EPS = 1e-5
NEG = -1e30
QK_SCALE = HEAD_DIM ** -0.5

LANES = 128
PAIR = 2 * HEAD_DIM
N_PAIRS = N_HEADS // 2
FF_CHUNK = 1408
PAGES_PER_STEP = 4
ATTN_BLOCKS_PER_TILE = 2
VMEM_LIMIT = 56 * 1024 * 1024

_NT = (((1,), (1,)), ((), ()))


def _dot(a, b):
    return jnp.dot(a, b, preferred_element_type=F32)


def _dot_nt(a, b):
    return lax.dot_general(a, b, _NT, preferred_element_type=F32)


def _split3(x):
    hi = x.astype(BF16)
    r1 = x - hi.astype(F32)
    mid = r1.astype(BF16)
    lo = (r1 - mid.astype(F32)).astype(BF16)
    return hi, mid, lo


def _dot_exact_lhs(a_bf, x):
    hi, mid, lo = _split3(x)
    return _dot(a_bf, hi) + _dot(a_bf, mid) + _dot(a_bf, lo)


def _layer_norm(z, g, b):
    mu = jnp.mean(z, axis=-1, keepdims=True)
    zc = z - mu
    var = jnp.mean(zc * zc, axis=-1, keepdims=True)
    return zc * lax.rsqrt(var + LN_EPS) * g + b


def _log_sigmoid(x):
    return jnp.minimum(x, 0.0) - jnp.log1p(jnp.exp(-jnp.abs(x)))


def _params(*sem):
    return pltpu.CompilerParams(dimension_semantics=sem, vmem_limit_bytes=VMEM_LIMIT)


def _ffn_body(x_ref, wg_ref, wu_ref, wd_ref, g_ref, b_ref, o_ref, acc_ref, *, n_chunks):
    c = pl.program_id(1)
    x = x_ref[...]
    xb = x.astype(BF16)
    gate = _dot(xb, wg_ref[...])
    up = _dot(xb, wu_ref[...])
    h = (gate * jax.nn.sigmoid(gate)) * up
    part = _dot(h.astype(BF16), wd_ref[...])

    @pl.when(c == 0)
    def _():
        acc_ref[...] = part

    @pl.when(c > 0)
    def _():
        acc_ref[...] += part

    @pl.when(c == n_chunks - 1)
    def _():
        z = DN_ALPHA * x + 0.5 * acc_ref[...]
        o_ref[...] = _layer_norm(z, g_ref[...], b_ref[...])


def _ffn(x, w_up, w_down, g, b, tm):
    n = x.shape[0]
    nc = D_FF // FF_CHUNK
    return pl.pallas_call(
        functools.partial(_ffn_body, n_chunks=nc),
        grid=(n // tm, nc),
        in_specs=[
            pl.BlockSpec((tm, D_MODEL), lambda i, c: (i, 0)),
            pl.BlockSpec((D_MODEL, FF_CHUNK), lambda i, c: (0, c)),
            pl.BlockSpec((D_MODEL, FF_CHUNK), lambda i, c: (0, c + D_FF // FF_CHUNK)),
            pl.BlockSpec((FF_CHUNK, D_MODEL), lambda i, c: (c, 0)),
            pl.BlockSpec((1, D_MODEL), lambda i, c: (0, 0)),
            pl.BlockSpec((1, D_MODEL), lambda i, c: (0, 0)),
        ],
        out_specs=pl.BlockSpec((tm, D_MODEL), lambda i, c: (i, 0)),
        out_shape=jax.ShapeDtypeStruct((n, D_MODEL), F32),
        scratch_shapes=[pltpu.VMEM((tm, D_MODEL), F32)],
        compiler_params=_params("parallel", "arbitrary"),
        name="ffn",
    )(x, w_up, w_up, w_down, g, b)


def _outproj_t_body(y_ref, mt_ref, w_ref, g_ref, b_ref, o_ref, *, nb):
    for i in range(nb):
        rows = slice(i * MOBA_BLOCK, (i + 1) * MOBA_BLOCK)
        sub = lax.dot_general(mt_ref[i], w_ref[...], (((0,), (0,)), ((), ())),
                              preferred_element_type=F32)
        z = DN_ALPHA * y_ref[rows, :] + sub
        o_ref[rows, :] = _layer_norm(z, g_ref[...], b_ref[...])


def _outproj_t(y, mt, w_out, g, b, tm):
    n = y.shape[0]
    nb = tm // MOBA_BLOCK
    return pl.pallas_call(
        functools.partial(_outproj_t_body, nb=nb),
        grid=(n // tm,),
        in_specs=[
            pl.BlockSpec((tm, D_MODEL), lambda i: (i, 0)),
            pl.BlockSpec((nb, D_MODEL, MOBA_BLOCK), lambda i: (i, 0, 0)),
            pl.BlockSpec((D_MODEL, D_MODEL), lambda i: (0, 0)),
            pl.BlockSpec((1, D_MODEL), lambda i: (0, 0)),
            pl.BlockSpec((1, D_MODEL), lambda i: (0, 0)),
        ],
        out_specs=pl.BlockSpec((tm, D_MODEL), lambda i: (i, 0)),
        out_shape=jax.ShapeDtypeStruct((n, D_MODEL), F32),
        compiler_params=_params("parallel"),
        name="outproj_t",
    )(y, mt, w_out, g, b)


def _outproj_body(y_ref, m_ref, w_ref, g_ref, b_ref, o_ref):
    sub = _dot(m_ref[...].astype(BF16), w_ref[...])
    z = DN_ALPHA * y_ref[...] + sub
    o_ref[...] = _layer_norm(z, g_ref[...], b_ref[...])


def _outproj(y, m, w_out, g, b):
    n = y.shape[0]
    full = lambda shape: pl.BlockSpec(shape, lambda i: (0, 0))
    return pl.pallas_call(
        _outproj_body,
        grid=(1,),
        in_specs=[full((n, D_MODEL)), full((n, D_MODEL)), full((D_MODEL, D_MODEL)),
                  full((1, D_MODEL)), full((1, D_MODEL))],
        out_specs=full((n, D_MODEL)),
        out_shape=jax.ShapeDtypeStruct((n, D_MODEL), F32),
        compiler_params=_params("arbitrary"),
        name="outproj",
    )(y, m, w_out, g, b)


def _proj_common(y_ref, wk_ref, wkt_ref, wqt_ref, wvt_ref, k_ref, v_ref, kb_ref, qt_ref, vt_ref, nb):
    yb = y_ref[...].astype(BF16)
    k = _dot(yb, wk_ref[...])
    k_ref[0] = _dot_nt(wkt_ref[...], yb)
    kb = k.astype(BF16)
    qt = _dot_nt(wqt_ref[...], yb)
    vt = _dot_nt(wvt_ref[...], yb)
    v_ref[0] = vt
    for i in range(nb):
        cols = slice(i * MOBA_BLOCK, (i + 1) * MOBA_BLOCK)
        kb_ref[i] = kb[cols, :]
        qt_ref[i] = (qt[:, cols] * QK_SCALE).astype(BF16)
        vt_ref[i] = vt[:, cols].astype(BF16)
    return k, qt


def _proj_moba_body(y_ref, wk_ref, wkt_ref, wqt_ref, wvt_ref, hm_ref,
                    k_ref, v_ref, kb_ref, qt_ref, vt_ref, sel_ref, km_ref, *, nb, n_blk):
    t = pl.program_id(1)

    @pl.when(t == 0)
    def _():
        km_ref[...] = jnp.zeros_like(km_ref)

    k, qt = _proj_common(y_ref, wk_ref, wkt_ref, wqt_ref, wvt_ref,
                         k_ref, v_ref, kb_ref, qt_ref, vt_ref, nb)
    for i in range(nb):
        rows = slice(i * MOBA_BLOCK, (i + 1) * MOBA_BLOCK)
        km_ref[pl.ds(t * nb + i, 1), :] = jnp.sum(k[rows, :], axis=0, keepdims=True) / MOBA_BLOCK

    km = km_ref[...]
    hm = hm_ref[...]
    kmb = jnp.concatenate(
        [jnp.broadcast_to(km[n:n + 1, :], (N_HEADS, D_MODEL)) * hm for n in range(n_blk)], axis=0)
    a_hi = kmb.astype(BF16)
    a_mid = (kmb - a_hi.astype(F32)).astype(BF16)
    q_hi = qt.astype(BF16)
    q_mid = (qt - q_hi.astype(F32)).astype(BF16)
    gate = _dot(a_hi, q_hi) + _dot(a_hi, q_mid) + _dot(a_mid, q_hi)

    tm = nb * MOBA_BLOCK
    col = lax.broadcasted_iota(jnp.int32, (N_HEADS, tm), 1)
    q_blk = t * nb + col // MOBA_BLOCK
    g = []
    for n in range(n_blk):
        g.append(jnp.where(n < q_blk, gate[n * N_HEADS:(n + 1) * N_HEADS, :], NEG))
    bias = []
    for n in range(n_blk):
        rank = jnp.zeros((N_HEADS, tm), jnp.int32)
        for m in range(n_blk):
            if m == n:
                continue
            ahead = (g[m] >= g[n]) if m < n else (g[m] > g[n])
            rank = rank + ahead.astype(jnp.int32)
        keep = (n < q_blk) & (rank < MOBA_TOPK)
        bias.append(jnp.where(keep, 0.0, NEG))
    bias = jnp.concatenate(bias, axis=0)
    for i in range(nb):
        sel_ref[i] = bias[:, i * MOBA_BLOCK:(i + 1) * MOBA_BLOCK]


def _proj_moba(y, wk, wkt, wqt, wvt, hm, batch, seq, tm):
    n = y.shape[0]
    nb = tm // MOBA_BLOCK
    n_blk = seq // MOBA_BLOCK
    tiles = seq // tm
    full = lambda shape: pl.BlockSpec(shape, lambda b, t: (0,) * len(shape))
    row = lambda b, t: (b * tiles + t, 0)
    blk = lambda b, t: (b * tiles + t, 0, 0)
    tcol = lambda b, t: (b, 0, t)
    nblocks = n // MOBA_BLOCK
    return pl.pallas_call(
        functools.partial(_proj_moba_body, nb=nb, n_blk=n_blk),
        grid=(batch, tiles),
        in_specs=[pl.BlockSpec((tm, D_MODEL), row), full((D_MODEL, D_MODEL)), full((D_MODEL, D_MODEL)),
                  full((D_MODEL, D_MODEL)), full((D_MODEL, D_MODEL)), full((N_HEADS, D_MODEL))],
        out_specs=[
            pl.BlockSpec((1, D_MODEL, tm), tcol),
            pl.BlockSpec((1, D_MODEL, tm), tcol),
            pl.BlockSpec((nb, MOBA_BLOCK, D_MODEL), blk),
            pl.BlockSpec((nb, D_MODEL, MOBA_BLOCK), blk),
            pl.BlockSpec((nb, D_MODEL, MOBA_BLOCK), blk),
            pl.BlockSpec((nb, n_blk * N_HEADS, MOBA_BLOCK), blk),
        ],
        out_shape=[
            jax.ShapeDtypeStruct((batch, D_MODEL, seq), F32),
            jax.ShapeDtypeStruct((batch, D_MODEL, seq), F32),
            jax.ShapeDtypeStruct((nblocks, MOBA_BLOCK, D_MODEL), BF16),
            jax.ShapeDtypeStruct((nblocks, D_MODEL, MOBA_BLOCK), BF16),
            jax.ShapeDtypeStruct((nblocks, D_MODEL, MOBA_BLOCK), BF16),
            jax.ShapeDtypeStruct((nblocks, n_blk * N_HEADS, MOBA_BLOCK), F32),
        ],
        scratch_shapes=[pltpu.VMEM((n_blk, D_MODEL), F32)],
        compiler_params=_params("arbitrary", "arbitrary"),
        name="proj_moba",
    )(y, wk, wkt, wqt, wvt, hm)


def _logf_from(yb, wf_ref, bf_ref, tm):
    lane = lax.broadcasted_iota(jnp.int32, (tm, LANES), 1)
    zf = _dot(yb, wf_ref[...]) + bf_ref[...]
    return jnp.where(lane < N_HEADS, _log_sigmoid(zf), 0.0)


def _pack3(x):
    hi, mid, lo = _split3(x)
    packed = (hi.astype(F32) + pltpu.roll(mid.astype(F32), N_HEADS, 1)
              + pltpu.roll(lo.astype(F32), 2 * N_HEADS, 1))
    return packed.astype(BF16)


def _proj_fox_body(y_ref, wk_ref, wkt_ref, wqt_ref, wvt_ref, wf_ref, bf_ref, tri_ref, place_ref,
                   k_ref, v_ref, kb_ref, qt_ref, vt_ref, lf_ref, ck_ref, carry_ref, *, nb):
    t = pl.program_id(1)

    @pl.when(t == 0)
    def _():
        carry_ref[...] = jnp.zeros_like(carry_ref)

    _proj_common(y_ref, wk_ref, wkt_ref, wqt_ref, wvt_ref, k_ref, v_ref, kb_ref, qt_ref, vt_ref, nb)
    tm = nb * MOBA_BLOCK
    yb = y_ref[...].astype(BF16)
    lf = _logf_from(yb, wf_ref, bf_ref, tm)
    lf_ref[...] = lf[:, :N_HEADS]
    ct = _dot_exact_lhs(tri_ref[...], lf) + carry_ref[0:1, :]
    carry_ref[0:1, :] = ct[tm - 1:tm, :]
    ck = _dot(_pack3(ct), place_ref[...]).astype(BF16)
    for i in range(nb):
        ck_ref[i] = ck[i * MOBA_BLOCK:(i + 1) * MOBA_BLOCK, :]


def _proj_fox(y, wk, wkt, wqt, wvt, wf, bf, tri, place, batch, seq, tm):
    n = y.shape[0]
    nb = tm // MOBA_BLOCK
    tiles = seq // tm
    full = lambda shape: pl.BlockSpec(shape, lambda b, t: (0,) * len(shape))
    row = lambda b, t: (b * tiles + t, 0)
    blk = lambda b, t: (b * tiles + t, 0, 0)
    tcol = lambda b, t: (b, 0, t)
    nblocks = n // MOBA_BLOCK
    return pl.pallas_call(
        functools.partial(_proj_fox_body, nb=nb),
        grid=(batch, tiles),
        in_specs=[pl.BlockSpec((tm, D_MODEL), row), full((D_MODEL, D_MODEL)), full((D_MODEL, D_MODEL)),
                  full((D_MODEL, D_MODEL)), full((D_MODEL, D_MODEL)), full((D_MODEL, LANES)),
                  full((1, LANES)), full((tm, tm)), full((LANES, D_MODEL))],
        out_specs=[
            pl.BlockSpec((1, D_MODEL, tm), tcol),
            pl.BlockSpec((1, D_MODEL, tm), tcol),
            pl.BlockSpec((nb, MOBA_BLOCK, D_MODEL), blk),
            pl.BlockSpec((nb, D_MODEL, MOBA_BLOCK), blk),
            pl.BlockSpec((nb, D_MODEL, MOBA_BLOCK), blk),
            pl.BlockSpec((tm, N_HEADS), row),
            pl.BlockSpec((nb, MOBA_BLOCK, D_MODEL), blk),
        ],
        out_shape=[
            jax.ShapeDtypeStruct((batch, D_MODEL, seq), F32),
            jax.ShapeDtypeStruct((batch, D_MODEL, seq), F32),
            jax.ShapeDtypeStruct((nblocks, MOBA_BLOCK, D_MODEL), BF16),
            jax.ShapeDtypeStruct((nblocks, D_MODEL, MOBA_BLOCK), BF16),
            jax.ShapeDtypeStruct((nblocks, D_MODEL, MOBA_BLOCK), BF16),
            jax.ShapeDtypeStruct((n, N_HEADS), F32),
            jax.ShapeDtypeStruct((nblocks, MOBA_BLOCK, D_MODEL), BF16),
        ],
        scratch_shapes=[pltpu.VMEM((8, LANES), F32)],
        compiler_params=_params("arbitrary", "arbitrary"),
        name="proj_fox",
    )(y, wk, wkt, wqt, wvt, wf, bf, tri, place)


def _proj_nat_body(y_ref, wq_ref, wk_ref, wv_ref, *rest, with_logf):
    if with_logf:
        wf_ref, bf_ref, q_ref, k_ref, v_ref, lf_ref = rest
    else:
        q_ref, k_ref, v_ref = rest
    yb = y_ref[...].astype(BF16)
    q_ref[...] = _dot(yb, wq_ref[...])
    k_ref[...] = _dot(yb, wk_ref[...])
    v_ref[...] = _dot(yb, wv_ref[...])
    if with_logf:
        lf_ref[...] = _logf_from(yb, wf_ref, bf_ref, y_ref.shape[0])[:, :N_HEADS]


def _proj_nat(y, wq, wk, wv, wf=None, bf=None):
    n = y.shape[0]
    with_logf = wf is not None
    full = lambda shape: pl.BlockSpec(shape, lambda i: (0,) * len(shape))
    ins = [y, wq, wk, wv]
    in_specs = [full((n, D_MODEL))] + [full((D_MODEL, D_MODEL))] * 3
    out_specs = [full((n, D_MODEL))] * 3
    out_shape = [jax.ShapeDtypeStruct((n, D_MODEL), F32)] * 3
    if with_logf:
        ins += [wf, bf]
        in_specs += [full((D_MODEL, LANES)), full((1, LANES))]
        out_specs += [full((n, N_HEADS))]
        out_shape += [jax.ShapeDtypeStruct((n, N_HEADS), F32)]
    return pl.pallas_call(
        functools.partial(_proj_nat_body, with_logf=with_logf),
        grid=(1,),
        in_specs=in_specs,
        out_specs=out_specs,
        out_shape=out_shape,
        compiler_params=_params("arbitrary"),
        name="proj_nat",
    )(*ins)


def _attn_body(kb_ref, aux_ref, qt_ref, vt_ref, qa_ref, *rest, n_blk, use_sel, sb):
    if use_sel:
        sel_ref, o_ref = rest
    else:
        (o_ref,) = rest
    hp = pl.program_id(1)
    tile = sb * MOBA_BLOCK
    key_i = lax.broadcasted_iota(jnp.int32, (tile, tile), 0)
    qry_i = lax.broadcasted_iota(jnp.int32, (tile, tile), 1)
    causal = key_i <= qry_i
    rid = lax.broadcasted_iota(jnp.int32, (PAIR, tile), 0)
    hrows = [slice(i * HEAD_DIM, (i + 1) * HEAD_DIM) for i in range(2)]

    def cat(parts, axis):
        return parts[0] if len(parts) == 1 else jnp.concatenate(parts, axis=axis)

    def keys_of(t):
        kb = cat([kb_ref[t * sb + a] for a in range(sb)], 0)
        aux = cat([aux_ref[t * sb + a] for a in range(sb)], 0)
        return jnp.concatenate([kb, aux], axis=1)

    def vt_of(t, i):
        return cat([vt_ref[t * sb + a, hrows[i], :] for a in range(sb)], 1)

    def sel_rows(jt, kt, i, a, first_c):
        row = (kt * sb + a) * N_HEADS + 2 * hp + i
        parts = [sel_ref[jt * sb + c, pl.ds(row, 1), :] if c >= first_c
                 else jnp.zeros((1, MOBA_BLOCK), F32) for c in range(sb)]
        return cat(parts, 1)

    def add_sel(s, jt, kt, i, diag):
        if not use_sel:
            return s
        rows = []
        for a in range(sb):
            sa = s[a * MOBA_BLOCK:(a + 1) * MOBA_BLOCK, :]
            if diag and a == sb - 1:
                rows.append(sa)
            else:
                rows.append(sa + sel_rows(jt, kt, i, a, a + 1 if diag else 0))
        return cat(rows, 0)

    def q_tile(jt, carry):
        qt = cat([qt_ref[jt * sb + c] for c in range(sb)], 1)
        qaug = []
        for i in range(2):
            own = (rid >= i * HEAD_DIM) & (rid < (i + 1) * HEAD_DIM)
            qa = cat([qa_ref[i]] * sb, 1)
            qaug.append(jnp.concatenate([jnp.where(own, qt, jnp.zeros_like(qt)), qa], axis=0))

        kcat = keys_of(jt)
        state = []
        for i in range(2):
            s = add_sel(_dot(kcat, qaug[i]), jt, jt, i, True)
            s = jnp.where(causal, s, NEG)
            m = jnp.max(s, axis=0, keepdims=True)
            p = jnp.exp(s - m)
            l = jnp.sum(p, axis=0, keepdims=True)
            state += [m, l, _dot(vt_of(jt, i), p.astype(BF16))]

        def past(kt, c):
            kcat = keys_of(kt)
            new = []
            for i in range(2):
                m, l, acc = c[3 * i:3 * i + 3]
                s = add_sel(_dot(kcat, qaug[i]), jt, kt, i, False)
                m_new = jnp.maximum(m, jnp.max(s, axis=0, keepdims=True))
                corr = jnp.exp(m - m_new)
                p = jnp.exp(s - m_new)
                l = l * corr + jnp.sum(p, axis=0, keepdims=True)
                acc = acc * corr + _dot(vt_of(kt, i), p.astype(BF16))
                new += [m_new, l, acc]
            return tuple(new)

        state = lax.fori_loop(0, jt, past, tuple(state))
        for i in range(2):
            m, l, acc = state[3 * i:3 * i + 3]
            out = (acc / l).astype(BF16)
            for c in range(sb):
                o_ref[jt * sb + c, hrows[i], :] = out[:, c * MOBA_BLOCK:(c + 1) * MOBA_BLOCK]
        return carry

    lax.fori_loop(0, n_blk // sb, q_tile, 0)


def _attn(kb, aux, qt, vt, qa, sel, batch, seq, aux_is_const):
    n_blk = seq // MOBA_BLOCK
    nblocks = kb.shape[0]
    use_sel = sel is not None
    aux_map = (lambda b, hp: (0, 0, 0)) if aux_is_const else (lambda b, hp: (b, 0, hp))
    in_specs = [
        pl.BlockSpec((n_blk, MOBA_BLOCK, PAIR), lambda b, hp: (b, 0, hp)),
        pl.BlockSpec((n_blk, MOBA_BLOCK, PAIR), aux_map),
        pl.BlockSpec((n_blk, PAIR, MOBA_BLOCK), lambda b, hp: (b, hp, 0)),
        pl.BlockSpec((n_blk, PAIR, MOBA_BLOCK), lambda b, hp: (b, hp, 0)),
        pl.BlockSpec((2, PAIR, MOBA_BLOCK), lambda b, hp: (hp, 0, 0)),
    ]
    ins = [kb, aux, qt, vt, qa]
    if use_sel:
        in_specs.append(pl.BlockSpec((n_blk, n_blk * N_HEADS, MOBA_BLOCK), lambda b, hp: (b, 0, 0)))
        ins.append(sel)
    return pl.pallas_call(
        functools.partial(_attn_body, n_blk=n_blk, use_sel=use_sel, sb=ATTN_BLOCKS_PER_TILE),
        grid=(batch, N_PAIRS),
        in_specs=in_specs,
        out_specs=pl.BlockSpec((n_blk, PAIR, MOBA_BLOCK), lambda b, hp: (b, hp, 0)),
        out_shape=jax.ShapeDtypeStruct((nblocks, D_MODEL, MOBA_BLOCK), BF16),
        compiler_params=_params("parallel", "parallel"),
        name="attn_moba" if use_sel else "attn_fox",
    )(*ins)


def _diag_rows(x, t):
    return jnp.sum(x.reshape(N_HEADS, t, D_MODEL), axis=0)


def _dot_exact_rhs(x, b_bf):
    hi, mid, lo = _split3(x)
    return _dot(hi, b_bf) + _dot(mid, b_bf) + _dot(lo, b_bf)


def _q_rows(q_ref, dm, scale):
    return jnp.concatenate([q_ref[...]] * N_HEADS, axis=0) * (dm * scale)


def _expand(c, dm, t):
    return _diag_rows(jnp.broadcast_to(c, dm.shape) * dm, t)


def _new_token_scores(qbt, kn_ref, vn_ref, nbias, extra_cols, t_new):
    pad = jnp.zeros((LANES - t_new, D_MODEL), F32)
    kn = jnp.concatenate([kn_ref[...], pad], axis=0).astype(BF16)
    vn = jnp.concatenate([vn_ref[...], pad], axis=0).astype(BF16)
    if extra_cols is not None:
        kn = jnp.concatenate([kn, extra_cols], axis=1)
    return _dot_nt(qbt, kn) + nbias, vn


def _moba_sample_body(pt_ref, q_ref, kn_ref, vn_ref, dm_ref, slb_ref, qp_ref, nbias_ref,
                      *rest, pp, t_new, n_blk):
    del pt_ref
    k_refs, v_refs = rest[:pp], rest[pp:2 * pp]
    o_ref, qbt_ref, qmid_ref, g_ref, m_ref, l_ref, od_ref = rest[2 * pp:]
    s = pl.program_id(1)
    rows = N_HEADS * t_new
    dm = dm_ref[...]
    lane = lax.broadcasted_iota(jnp.int32, (rows, LANES), 1)

    @pl.when(s == 0)
    def _():
        q16 = _q_rows(q_ref, dm, QK_SCALE)
        hi = q16.astype(BF16)
        qbt_ref[...] = hi
        qmid_ref[...] = (q16 - hi.astype(F32)).astype(BF16)
        g_ref[...] = jnp.zeros_like(g_ref)
        m_ref[...] = jnp.zeros_like(m_ref)
        l_ref[...] = jnp.zeros_like(l_ref)

    qbt = qbt_ref[...]
    per_blk = MOBA_BLOCK // PAGE_SIZE
    for i in range(pp // per_blk):
        n = s * (pp // per_blk) + i
        kf = jnp.concatenate([k_refs[per_blk * i + j][...] for j in range(per_blk)], axis=1)
        kt = kf.astype(BF16)
        vt = jnp.concatenate([v_refs[per_blk * i + j][...] for j in range(per_blk)], axis=1).astype(BF16)
        sc = _dot(qbt, kt)
        kt_mid = (kf - kt.astype(F32)).astype(BF16)
        gsum = (jnp.sum(sc, axis=1, keepdims=True)
                + jnp.sum(_dot(qmid_ref[...], kt), axis=1, keepdims=True)
                + jnp.sum(_dot(qbt, kt_mid), axis=1, keepdims=True))
        dist = qp_ref[...] - (n * MOBA_BLOCK).astype(F32)
        sc = sc - slb_ref[...] * dist
        m = jnp.max(sc, axis=1, keepdims=True)
        e = jnp.exp(sc - m)
        l = jnp.sum(e, axis=1, keepdims=True)
        o = _dot_nt(e.astype(BF16), vt)
        od_ref[n] = _diag_rows(o * dm, t_new)
        onehot = (lane == n).astype(F32)
        g_ref[...] += gsum * onehot
        m_ref[...] += m * onehot
        l_ref[...] += l * onehot

    @pl.when(s == pl.num_programs(1) - 1)
    def _():
        low = jnp.float32(-3.0e38)
        g = jnp.where(lane < n_blk, g_ref[...], low)
        lane_f = lane.astype(F32)
        sel = lane < 0
        for _ in range(MOBA_TOPK):
            mx = jnp.max(g, axis=1, keepdims=True)
            first = jnp.min(jnp.where(g == mx, lane_f, float(LANES)), axis=1, keepdims=True)
            pick = lane_f == first
            sel = sel | pick
            g = jnp.where(pick, low, g)
        sn, vn = _new_token_scores(qbt, kn_ref, vn_ref, nbias_ref[...], None, t_new)
        m_n = jnp.max(sn, axis=1, keepdims=True)
        e_n = jnp.exp(sn - m_n)
        l_n = jnp.sum(e_n, axis=1, keepdims=True)
        od_n = _diag_rows(_dot(e_n.astype(BF16), vn) * dm, t_new)
        m_all = m_ref[...]
        m_tot = jnp.maximum(jnp.max(jnp.where(sel, m_all, NEG), axis=1, keepdims=True), m_n)
        w = jnp.where(sel, jnp.exp(jnp.minimum(m_all - m_tot, 0.0)), 0.0)
        w_n = jnp.exp(m_n - m_tot)
        l_tot = jnp.sum(w * l_ref[...], axis=1, keepdims=True) + w_n * l_n
        acc = _expand(w_n, dm, t_new) * od_n
        for n in range(n_blk):
            acc = acc + _expand(w[:, n:n + 1], dm, t_new) * od_ref[n]
        o_ref[...] = acc / _expand(l_tot, dm, t_new)


def _page_spec(i, pp, n_pages, reverse):
    def index(b, s, pt):
        p = s * pp + i
        if reverse:
            p = n_pages - 1 - p
        return (pt[b * n_pages + p], 0, 0)
    return index


def _moba_sample(q, k_new, v_new, cache_kt, cache_vt, page_table, consts, t_new):
    batch = q.shape[0] // t_new
    n_pages = page_table.shape[1]
    n_blk = n_pages * PAGE_SIZE // MOBA_BLOCK
    pp = PAGES_PER_STEP
    rows = N_HEADS * t_new
    tok = pl.BlockSpec((t_new, D_MODEL), lambda b, s, pt: (b, 0))
    const = lambda a: pl.BlockSpec(a.shape, lambda b, s, pt: (0, 0))
    page = lambda i: pl.BlockSpec((None, D_MODEL, PAGE_SIZE), _page_spec(i, pp, n_pages, False))
    grid_spec = pltpu.PrefetchScalarGridSpec(
        num_scalar_prefetch=1,
        grid=(batch, n_pages // pp),
        in_specs=[tok, tok, tok] + [const(a) for a in consts]
        + [page(i) for i in range(pp)] + [page(i) for i in range(pp)],
        out_specs=tok,
        scratch_shapes=[
            pltpu.VMEM((rows, D_MODEL), BF16),
            pltpu.VMEM((rows, D_MODEL), BF16),
            pltpu.VMEM((rows, LANES), F32),
            pltpu.VMEM((rows, LANES), F32),
            pltpu.VMEM((rows, LANES), F32),
            pltpu.VMEM((n_blk, t_new, D_MODEL), F32),
        ],
    )
    return pl.pallas_call(
        functools.partial(_moba_sample_body, pp=pp, t_new=t_new, n_blk=n_blk),
        grid_spec=grid_spec,
        out_shape=jax.ShapeDtypeStruct((batch * t_new, D_MODEL), F32),
        compiler_params=_params("parallel", "arbitrary"),
        name="moba_sample",
    )(page_table.reshape(-1), q, k_new, v_new, *consts,
      *([cache_kt] * pp), *([cache_vt] * pp))


def _fox_sample_body(pt_ref, q_ref, kn_ref, vn_ref, lfn_ref, dm_ref, qaux_ref, tri_ref, nbias_ref,
                     later_ref, ones_ref, *rest, pp, t_new):
    del pt_ref
    k_refs, v_refs, lf_refs = rest[:pp], rest[pp:2 * pp], rest[2 * pp:3 * pp]
    o_ref, qbt_ref, pad_ref, carry_ref, m_ref, l_ref, acc_ref = rest[3 * pp:]
    s = pl.program_id(1)
    dm = dm_ref[...]

    @pl.when(s == 0)
    def _():
        qbt_ref[:, :D_MODEL] = _q_rows(q_ref, dm, QK_SCALE).astype(BF16)
        qbt_ref[:, D_MODEL:] = qaux_ref[...]
        carry_ref[...] = jnp.zeros_like(carry_ref)
        m_ref[...] = jnp.full_like(m_ref, NEG)
        l_ref[...] = jnp.zeros_like(l_ref)
        acc_ref[...] = jnp.zeros_like(acc_ref)

    qbt = qbt_ref[...]

    def online(sc, weigh):
        m_old = m_ref[:, 0:1]
        m_new = jnp.maximum(m_old, jnp.max(sc, axis=1, keepdims=True))
        corr = jnp.exp(m_old - m_new)
        e = jnp.exp(sc - m_new)
        l_ref[...] = jnp.broadcast_to(l_ref[:, 0:1] * corr + jnp.sum(e, axis=1, keepdims=True), l_ref.shape)
        acc_ref[...] = acc_ref[...] * corr + weigh(e.astype(BF16))
        m_ref[...] = jnp.broadcast_to(m_new, m_ref.shape)

    zrows = jnp.zeros((LANES - 3 * N_HEADS, PAGE_SIZE), BF16)
    for i in range(pp):
        lft = lf_refs[i][...]
        suf = _dot_exact_rhs(lft, later_ref[...]) + carry_ref[...]
        carry_ref[...] = carry_ref[...] + _dot_exact_rhs(lft, ones_ref[...])
        hi, mid, lo = _split3(suf)
        kcat = jnp.concatenate([k_refs[i][...].astype(BF16), hi, mid, lo, zrows], axis=0)
        vt = v_refs[i][...].astype(BF16)
        online(_dot(qbt, kcat), lambda e, vt=vt: _dot_nt(e, vt))

    @pl.when(s == pl.num_programs(1) - 1)
    def _():
        pad_ref[...] = jnp.zeros_like(pad_ref)
        pad_ref[0:t_new, :N_HEADS] = lfn_ref[...]
        c_new = _dot_exact_lhs(tri_ref[...], pad_ref[...])
        sn, vn = _new_token_scores(qbt, kn_ref, vn_ref, nbias_ref[...], _pack3(-c_new), t_new)
        online(sn, lambda e: _dot(e, vn))
        out = acc_ref[...] / l_ref[:, 0:1]
        o_ref[...] = _diag_rows(out * dm, t_new)


def _fox_sample(q, k_new, v_new, lf_new, cache_kt, cache_vt, cache_lft, page_table, consts, t_new):
    batch = q.shape[0] // t_new
    n_pages = page_table.shape[1]
    pp = PAGES_PER_STEP
    rows = N_HEADS * t_new
    tok = pl.BlockSpec((t_new, D_MODEL), lambda b, s, pt: (b, 0))
    const = lambda a: pl.BlockSpec(a.shape, lambda b, s, pt: (0, 0))
    page = lambda i: pl.BlockSpec((None, D_MODEL, PAGE_SIZE), _page_spec(i, pp, n_pages, True))
    lfpage = lambda i: pl.BlockSpec((None, N_HEADS, PAGE_SIZE), _page_spec(i, pp, n_pages, True))
    grid_spec = pltpu.PrefetchScalarGridSpec(
        num_scalar_prefetch=1,
        grid=(batch, n_pages // pp),
        in_specs=[tok, tok, tok, pl.BlockSpec((t_new, N_HEADS), lambda b, s, pt: (b, 0))]
        + [const(a) for a in consts]
        + [page(i) for i in range(pp)] + [page(i) for i in range(pp)] + [lfpage(i) for i in range(pp)],
        out_specs=tok,
        scratch_shapes=[
            pltpu.VMEM((rows, D_MODEL + LANES), BF16),
            pltpu.VMEM((LANES, LANES), F32),
            pltpu.VMEM((N_HEADS, PAGE_SIZE), F32),
            pltpu.VMEM((rows, LANES), F32),
            pltpu.VMEM((rows, LANES), F32),
            pltpu.VMEM((rows, D_MODEL), F32),
        ],
    )
    return pl.pallas_call(
        functools.partial(_fox_sample_body, pp=pp, t_new=t_new),
        grid_spec=grid_spec,
        out_shape=jax.ShapeDtypeStruct((batch * t_new, D_MODEL), F32),
        compiler_params=_params("parallel", "arbitrary"),
        name="fox_sample",
    )(page_table.reshape(-1), q, k_new, v_new, lf_new, *consts,
      *([cache_kt] * pp), *([cache_vt] * pp), *([cache_lft] * pp))


def _np_split3(x):
    x = jnp.asarray(x, F32)
    hi, mid, lo = _split3(x)
    return hi, mid, lo


def _alibi_slopes():
    return 2.0 ** (-8.0 * np.arange(1, N_HEADS + 1) / N_HEADS)


def _prompt_tables(seq, tm):
    n_blk = seq // MOBA_BLOCK
    pos = np.zeros((n_blk, MOBA_BLOCK, PAIR), np.float32)
    pos[:, :, 0:3] = np.arange(MOBA_BLOCK, dtype=np.float32)[None, :, None]
    pos[:, :, 3:6] = (np.arange(n_blk, dtype=np.float32) * MOBA_BLOCK)[:, None, None]
    s_hi, s_mid, s_lo = _np_split3(_alibi_slopes())
    parts = jnp.stack([s_hi, s_mid, s_lo, s_hi, s_mid, s_lo], axis=1).astype(F32)
    qa_moba = jnp.zeros((N_HEADS, PAIR, MOBA_BLOCK), F32)
    qa_moba = qa_moba.at[:, 0:6, :].set(jnp.broadcast_to(parts[:, :, None], (N_HEADS, 6, MOBA_BLOCK)))
    qa_fox = np.zeros((N_HEADS, PAIR, MOBA_BLOCK), np.float32)
    for h in range(N_HEADS):
        i = h % 2
        qa_fox[h, 3 * i:3 * i + 3, :] = -1.0
    place = np.zeros((LANES, D_MODEL), np.float32)
    for h in range(N_HEADS):
        for piece in range(3):
            place[piece * N_HEADS + h, (h // 2) * PAIR + 3 * (h % 2) + piece] = 1.0
    tri = np.tril(np.ones((tm, tm), np.float32))
    hm = (np.arange(D_MODEL)[None, :] // HEAD_DIM == np.arange(N_HEADS)[:, None]).astype(np.float32)
    return dict(pos=jnp.asarray(pos, BF16), qa_moba=qa_moba.astype(BF16), qa_fox=jnp.asarray(qa_fox, BF16),
                place=jnp.asarray(place, BF16), tri=jnp.asarray(tri, BF16), hm=jnp.asarray(hm, F32))


def _sample_tables(t_new, past_len):
    rows = N_HEADS * t_new
    h_of = np.arange(rows) // t_new
    t_of = np.arange(rows) % t_new
    dm = (np.arange(D_MODEL)[None, :] // HEAD_DIM == h_of[:, None]).astype(np.float32)
    slopes = _alibi_slopes().astype(np.float32)
    slb = np.broadcast_to(slopes[h_of][:, None], (rows, MOBA_BLOCK)).astype(np.float32)
    qp = (past_len + t_of[:, None] - np.arange(MOBA_BLOCK)[None, :]).astype(np.float32)
    s_idx = np.arange(LANES)[None, :]
    visible = (s_idx <= t_of[:, None]) & (s_idx < t_new)
    nb_moba = np.where(visible, -slopes[h_of][:, None] * (t_of[:, None] - s_idx), NEG).astype(np.float32)
    nb_fox = np.where(visible, 0.0, NEG).astype(np.float32)
    qaux = np.zeros((rows, LANES), np.float32)
    for piece in range(3):
        qaux[np.arange(rows), piece * N_HEADS + h_of] = 1.0
    tri = np.tril(np.ones((LANES, LANES), np.float32))
    later = np.tril(np.ones((PAGE_SIZE, PAGE_SIZE), np.float32), k=-1)
    ones = np.ones((PAGE_SIZE, PAGE_SIZE), np.float32)
    bf = lambda a: jnp.asarray(a, BF16)
    f32 = lambda a: jnp.asarray(a, F32)
    return dict(
        moba=(f32(dm), f32(slb), f32(qp), f32(nb_moba)),
        fox=(f32(dm), bf(qaux), bf(tri), f32(nb_fox), bf(later), bf(ones)))


def kernel(x_prompt, x_sample, cache_k_moba, cache_v_moba, cache_k_fox, cache_v_fox, cache_logf_fox,
           page_table, w_in_moba, w_out_moba, w_in_fox, b_f_fox, w_out_fox, w_ffn_up, w_ffn_down,
           ln_g, ln_b):
    batch, seq, d = x_prompt.shape
    dec_batch, t_new, _ = x_sample.shape
    n_pool = cache_k_moba.shape[1]
    past_len = page_table.shape[1] * PAGE_SIZE
    assert d == D_MODEL and seq % 512 == 0 and t_new == 8
    tm = 512
    pt = _prompt_tables(seq, tm)
    st = _sample_tables(t_new, past_len)

    yp = x_prompt.reshape(batch * seq, d)
    ys = x_sample.reshape(dec_batch * t_new, d)
    n_s = ys.shape[0]

    def norm(i, j):
        return ln_g[i, j].reshape(1, d), ln_b[i, j].reshape(1, d)

    def ffn_both(yp, ys, i, j):
        wu = w_ffn_up[i, j].astype(BF16)
        wd = w_ffn_down[i, j].astype(BF16)
        g, b = norm(i, 2 * j)
        return _ffn(yp, wu, wd, g, b, tm), _ffn(ys, wu, wd, g, b, n_s)

    def heads(z, lead):
        return z.reshape(lead + (N_HEADS, HEAD_DIM))[None]

    def heads_t(zt):
        return jnp.transpose(zt.reshape(batch, N_HEADS, HEAD_DIM, seq), (0, 3, 1, 2))[None]

    def pages_t(cache):
        return jnp.transpose(cache, (0, 2, 3, 1)).reshape(n_pool, d, PAGE_SIZE)

    outs = {}
    for i in range(DEPTH):
        li = i // 2
        yp, ys = ffn_both(yp, ys, i, 0)
        if i % 2 == 0:
            w = w_in_moba[li].astype(BF16)
            wq, wk, wv = w[:, :d], w[:, d:2 * d], w[:, 2 * d:3 * d]
            kp, vp, kb, qt, vt, sel = _proj_moba(yp, wk, wk.T, wq.T, wv.T, pt["hm"], batch, seq, tm)
            qs, ks, vs = _proj_nat(ys, wq, wk, wv)
            mt = _attn(kb, pt["pos"], qt, vt, pt["qa_moba"], sel, batch, seq, True)
            ms = _moba_sample(qs, ks, vs, pages_t(cache_k_moba[li]), pages_t(cache_v_moba[li]),
                              page_table, st["moba"], t_new)
            w_out = w_out_moba[li].astype(BF16)
            outs["kmp"], outs["vmp"] = heads_t(kp), heads_t(vp)
            outs["kms"], outs["vms"] = heads(ks, (dec_batch, t_new)), heads(vs, (dec_batch, t_new))
        else:
            w = w_in_fox[li].astype(BF16)
            wq, wk, wv = w[:, :d], w[:, d:2 * d], w[:, 2 * d:3 * d]
            wf = jnp.pad(w[:, 3 * d:], ((0, 0), (0, LANES - N_HEADS)))
            bf = jnp.pad(b_f_fox[li].astype(F32).reshape(1, N_HEADS), ((0, 0), (0, LANES - N_HEADS)))
            kp, vp, kb, qt, vt, lfp, ck = _proj_fox(yp, wk, wk.T, wq.T, wv.T, wf, bf, pt["tri"], pt["place"],
                                                   batch, seq, tm)
            qs, ks, vs, lfs = _proj_nat(ys, wq, wk, wv, wf, bf)
            mt = _attn(kb, ck, qt, vt, pt["qa_fox"], None, batch, seq, False)
            ms = _fox_sample(qs, ks, vs, lfs, pages_t(cache_k_fox[li]), pages_t(cache_v_fox[li]),
                             jnp.transpose(cache_logf_fox[li], (0, 2, 1)), page_table, st["fox"], t_new)
            w_out = w_out_fox[li].astype(BF16)
            outs["kfp"], outs["vfp"] = heads_t(kp), heads_t(vp)
            outs["lfp"] = lfp.reshape(batch, seq, N_HEADS)[None]
            outs["kfs"], outs["vfs"] = heads(ks, (dec_batch, t_new)), heads(vs, (dec_batch, t_new))
            outs["lfs"] = lfs.reshape(dec_batch, t_new, N_HEADS)[None]
        g, b = norm(i, 1)
        yp = _outproj_t(yp, mt, w_out, g, b, tm)
        ys = _outproj(ys, ms, w_out, g, b)
        yp, ys = ffn_both(yp, ys, i, 1)

    return (yp.reshape(batch, seq, d), ys.reshape(dec_batch, t_new, d),
            outs["kmp"], outs["vmp"], outs["kfp"], outs["vfp"], outs["lfp"],
            outs["kms"], outs["vms"], outs["kfs"], outs["vfs"], outs["lfs"])
```

```python
import functools

import numpy as np
import jax
import jax.numpy as jnp
from jax import lax
from jax.experimental import pallas as pl
from jax.experimental.pallas import tpu as pltpu

F32 = jnp.float32
BF16 = jnp.bfloat16

D_MODEL = 1024
N_HEADS = 16
HEAD_DIM = D_MODEL // N_HEADS
D_FF = 2816
DEPTH = 2
MOBA_BLOCK = 256
MOBA_TOPK = 3
PAGE_SIZE = 128
DN_ALPHA = (2.0 * DEPTH) ** 0.25
LN_EPS = 1e-5
NEG = -1e30
QK_SCALE = HEAD_DIM ** -0.5

LANES = 128
PAIR = 2 * HEAD_DIM
N_PAIRS = N_HEADS // 2
FF_CHUNK = 1408
PAGES_PER_STEP = 8
ATTN_BLOCKS_PER_TILE = 2
VMEM_LIMIT = 56 * 1024 * 1024

_NT = (((1,), (1,)), ((), ()))


def _dot(a, b):
    return jnp.dot(a, b, preferred_element_type=F32)


def _dot_nt(a, b):
    return lax.dot_general(a, b, _NT, preferred_element_type=F32)


def _split3(x):
    hi = x.astype(BF16)
    r1 = x - hi.astype(F32)
    mid = r1.astype(BF16)
    lo = (r1 - mid.astype(F32)).astype(BF16)
    return hi, mid, lo


def _dot_exact_lhs(a_bf, x):
    hi, mid, lo = _split3(x)
    return _dot(a_bf, hi) + _dot(a_bf, mid) + _dot(a_bf, lo)


def _layer_norm(z, g, b):
    mu = jnp.mean(z, axis=-1, keepdims=True)
    zc = z - mu
    var = jnp.mean(zc * zc, axis=-1, keepdims=True)
    return zc * lax.rsqrt(var + LN_EPS) * g + b


def _log_sigmoid(x):
    return jnp.minimum(x, 0.0) - jnp.log1p(jnp.exp(-jnp.abs(x)))


def _params(*sem):
    return pltpu.CompilerParams(dimension_semantics=sem, vmem_limit_bytes=VMEM_LIMIT)


def _ffn_body(x_ref, wg_ref, wu_ref, wd_ref, g_ref, b_ref, o_ref, acc_ref, *, n_chunks):
    c = pl.program_id(1)
    x = x_ref[...]
    xb = x.astype(BF16)
    gate = _dot(xb, wg_ref[...])
    up = _dot(xb, wu_ref[...])
    h = (gate * jax.nn.sigmoid(gate)) * up
    part = _dot(h.astype(BF16), wd_ref[...])

    @pl.when(c == 0)
    def _():
        acc_ref[...] = part

    @pl.when(c > 0)
    def _():
        acc_ref[...] += part

    @pl.when(c == n_chunks - 1)
    def _():
        z = DN_ALPHA * x + 0.5 * acc_ref[...]
        o_ref[...] = _layer_norm(z, g_ref[...], b_ref[...])


def _ffn(x, w_up, w_down, g, b, tm):
    n = x.shape[0]
    nc = D_FF // FF_CHUNK
    return pl.pallas_call(
        functools.partial(_ffn_body, n_chunks=nc),
        grid=(n // tm, nc),
        in_specs=[
            pl.BlockSpec((tm, D_MODEL), lambda i, c: (i, 0)),
            pl.BlockSpec((D_MODEL, FF_CHUNK), lambda i, c: (0, c)),
            pl.BlockSpec((D_MODEL, FF_CHUNK), lambda i, c: (0, c + D_FF // FF_CHUNK)),
            pl.BlockSpec((FF_CHUNK, D_MODEL), lambda i, c: (c, 0)),
            pl.BlockSpec((1, D_MODEL), lambda i, c: (0, 0)),
            pl.BlockSpec((1, D_MODEL), lambda i, c: (0, 0)),
        ],
        out_specs=pl.BlockSpec((tm, D_MODEL), lambda i, c: (i, 0)),
        out_shape=jax.ShapeDtypeStruct((n, D_MODEL), F32),
        scratch_shapes=[pltpu.VMEM((tm, D_MODEL), F32)],
        compiler_params=_params("parallel", "arbitrary"),
        name="ffn",
    )(x, w_up, w_up, w_down, g, b)


def _outproj_t_body(y_ref, mt_ref, w_ref, g_ref, b_ref, o_ref, *, nb):
    for i in range(nb):
        rows = slice(i * MOBA_BLOCK, (i + 1) * MOBA_BLOCK)
        sub = lax.dot_general(mt_ref[i], w_ref[...], (((0,), (0,)), ((), ())),
                              preferred_element_type=F32)
        z = DN_ALPHA * y_ref[rows, :] + sub
        o_ref[rows, :] = _layer_norm(z, g_ref[...], b_ref[...])


def _outproj_t(y, mt, w_out, g, b, tm):
    n = y.shape[0]
    nb = tm // MOBA_BLOCK
    return pl.pallas_call(
        functools.partial(_outproj_t_body, nb=nb),
        grid=(n // tm,),
        in_specs=[
            pl.BlockSpec((tm, D_MODEL), lambda i: (i, 0)),
            pl.BlockSpec((nb, D_MODEL, MOBA_BLOCK), lambda i: (i, 0, 0)),
            pl.BlockSpec((D_MODEL, D_MODEL), lambda i: (0, 0)),
            pl.BlockSpec((1, D_MODEL), lambda i: (0, 0)),
            pl.BlockSpec((1, D_MODEL), lambda i: (0, 0)),
        ],
        out_specs=pl.BlockSpec((tm, D_MODEL), lambda i: (i, 0)),
        out_shape=jax.ShapeDtypeStruct((n, D_MODEL), F32),
        compiler_params=_params("parallel"),
        name="outproj_t",
    )(y, mt, w_out, g, b)


def _outproj_body(y_ref, m_ref, w_ref, g_ref, b_ref, o_ref):
    sub = _dot(m_ref[...].astype(BF16), w_ref[...])
    z = DN_ALPHA * y_ref[...] + sub
    o_ref[...] = _layer_norm(z, g_ref[...], b_ref[...])


def _outproj(y, m, w_out, g, b):
    n = y.shape[0]
    full = lambda shape: pl.BlockSpec(shape, lambda i: (0, 0))
    return pl.pallas_call(
        _outproj_body,
        grid=(1,),
        in_specs=[full((n, D_MODEL)), full((n, D_MODEL)), full((D_MODEL, D_MODEL)),
                  full((1, D_MODEL)), full((1, D_MODEL))],
        out_specs=full((n, D_MODEL)),
        out_shape=jax.ShapeDtypeStruct((n, D_MODEL), F32),
        compiler_params=_params("arbitrary"),
        name="outproj",
    )(y, m, w_out, g, b)


def _proj_common(y_ref, wk_ref, wkt_ref, wqt_ref, wvt_ref, k_ref, v_ref, kb_ref, qt_ref, vt_ref, nb):
    yb = y_ref[...].astype(BF16)
    k = _dot(yb, wk_ref[...])
    k_ref[0] = _dot_nt(wkt_ref[...], yb)
    kb = k.astype(BF16)
    qt = _dot_nt(wqt_ref[...], yb)
    vt = _dot_nt(wvt_ref[...], yb)
    v_ref[0] = vt
    for i in range(nb):
        cols = slice(i * MOBA_BLOCK, (i + 1) * MOBA_BLOCK)
        kb_ref[i] = kb[cols, :]
        qt_ref[i] = (qt[:, cols] * QK_SCALE).astype(BF16)
        vt_ref[i] = vt[:, cols].astype(BF16)
    return k, qt


def _proj_moba_body(y_ref, wk_ref, wkt_ref, wqt_ref, wvt_ref, hm_ref,
                    k_ref, v_ref, kb_ref, qt_ref, vt_ref, sel_ref, km_ref, *, nb, n_blk):
    t = pl.program_id(1)

    @pl.when(t == 0)
    def _():
        km_ref[...] = jnp.zeros_like(km_ref)

    k, qt = _proj_common(y_ref, wk_ref, wkt_ref, wqt_ref, wvt_ref,
                         k_ref, v_ref, kb_ref, qt_ref, vt_ref, nb)
    for i in range(nb):
        rows = slice(i * MOBA_BLOCK, (i + 1) * MOBA_BLOCK)
        km_ref[pl.ds(t * nb + i, 1), :] = jnp.sum(k[rows, :], axis=0, keepdims=True) / MOBA_BLOCK

    km = km_ref[...]
    hm = hm_ref[...]
    kmb = jnp.concatenate(
        [jnp.broadcast_to(km[n:n + 1, :], (N_HEADS, D_MODEL)) * hm for n in range(n_blk)], axis=0)
    a_hi = kmb.astype(BF16)
    a_mid = (kmb - a_hi.astype(F32)).astype(BF16)
    q_hi = qt.astype(BF16)
    q_mid = (qt - q_hi.astype(F32)).astype(BF16)
    gate = _dot(a_hi, q_hi) + _dot(a_hi, q_mid) + _dot(a_mid, q_hi)

    tm = nb * MOBA_BLOCK
    col = lax.broadcasted_iota(jnp.int32, (N_HEADS, tm), 1)
    q_blk = t * nb + col // MOBA_BLOCK
    g = []
    for n in range(n_blk):
        g.append(jnp.where(n < q_blk, gate[n * N_HEADS:(n + 1) * N_HEADS, :], NEG))
    bias = []
    for n in range(n_blk):
        rank = jnp.zeros((N_HEADS, tm), jnp.int32)
        for m in range(n_blk):
            if m == n:
                continue
            ahead = (g[m] >= g[n]) if m < n else (g[m] > g[n])
            rank = rank + ahead.astype(jnp.int32)
        keep = (n < q_blk) & (rank < MOBA_TOPK)
        bias.append(jnp.where(keep, 0.0, NEG))
    bias = jnp.concatenate(bias, axis=0)
    for i in range(nb):
        sel_ref[i] = bias[:, i * MOBA_BLOCK:(i + 1) * MOBA_BLOCK]


def _proj_moba(y, wk, wkt, wqt, wvt, hm, batch, seq, tm):
    n = y.shape[0]
    nb = tm // MOBA_BLOCK
    n_blk = seq // MOBA_BLOCK
    tiles = seq // tm
    full = lambda shape: pl.BlockSpec(shape, lambda b, t: (0,) * len(shape))
    row = lambda b, t: (b * tiles + t, 0)
    blk = lambda b, t: (b * tiles + t, 0, 0)
    tcol = lambda b, t: (b, 0, t)
    nblocks = n // MOBA_BLOCK
    return pl.pallas_call(
        functools.partial(_proj_moba_body, nb=nb, n_blk=n_blk),
        grid=(batch, tiles),
        in_specs=[pl.BlockSpec((tm, D_MODEL), row), full((D_MODEL, D_MODEL)), full((D_MODEL, D_MODEL)),
                  full((D_MODEL, D_MODEL)), full((D_MODEL, D_MODEL)), full((N_HEADS, D_MODEL))],
        out_specs=[
            pl.BlockSpec((1, D_MODEL, tm), tcol),
            pl.BlockSpec((1, D_MODEL, tm), tcol),
            pl.BlockSpec((nb, MOBA_BLOCK, D_MODEL), blk),
            pl.BlockSpec((nb, D_MODEL, MOBA_BLOCK), blk),
            pl.BlockSpec((nb, D_MODEL, MOBA_BLOCK), blk),
            pl.BlockSpec((nb, n_blk * N_HEADS, MOBA_BLOCK), blk),
        ],
        out_shape=[
            jax.ShapeDtypeStruct((batch, D_MODEL, seq), F32),
            jax.ShapeDtypeStruct((batch, D_MODEL, seq), F32),
            jax.ShapeDtypeStruct((nblocks, MOBA_BLOCK, D_MODEL), BF16),
            jax.ShapeDtypeStruct((nblocks, D_MODEL, MOBA_BLOCK), BF16),
            jax.ShapeDtypeStruct((nblocks, D_MODEL, MOBA_BLOCK), BF16),
            jax.ShapeDtypeStruct((nblocks, n_blk * N_HEADS, MOBA_BLOCK), F32),
        ],
        scratch_shapes=[pltpu.VMEM((n_blk, D_MODEL), F32)],
        compiler_params=_params("arbitrary", "arbitrary"),
        name="proj_moba",
    )(y, wk, wkt, wqt, wvt, hm)


def _logf_from(yb, wf_ref, bf_ref, tm):
    lane = lax.broadcasted_iota(jnp.int32, (tm, LANES), 1)
    zf = _dot(yb, wf_ref[...]) + bf_ref[...]
    return jnp.where(lane < N_HEADS, _log_sigmoid(zf), 0.0)


def _pack3(x):
    hi, mid, lo = _split3(x)
    packed = (hi.astype(F32) + pltpu.roll(mid.astype(F32), N_HEADS, 1)
              + pltpu.roll(lo.astype(F32), 2 * N_HEADS, 1))
    return packed.astype(BF16)


def _proj_fox_body(y_ref, wk_ref, wkt_ref, wqt_ref, wvt_ref, wf_ref, bf_ref, tri_ref, place_ref,
                   k_ref, v_ref, kb_ref, qt_ref, vt_ref, lf_ref, ck_ref, carry_ref, *, nb):
    t = pl.program_id(1)

    @pl.when(t == 0)
    def _():
        carry_ref[...] = jnp.zeros_like(carry_ref)

    _proj_common(y_ref, wk_ref, wkt_ref, wqt_ref, wvt_ref, k_ref, v_ref, kb_ref, qt_ref, vt_ref, nb)
    tm = nb * MOBA_BLOCK
    yb = y_ref[...].astype(BF16)
    lf = _logf_from(yb, wf_ref, bf_ref, tm)
    lf_ref[...] = lf[:, :N_HEADS]
    ct = _dot_exact_lhs(tri_ref[...], lf) + carry_ref[0:1, :]
    carry_ref[0:1, :] = ct[tm - 1:tm, :]
    ck = _dot(_pack3(ct), place_ref[...]).astype(BF16)
    for i in range(nb):
        ck_ref[i] = ck[i * MOBA_BLOCK:(i + 1) * MOBA_BLOCK, :]


def _proj_fox(y, wk, wkt, wqt, wvt, wf, bf, tri, place, batch, seq, tm):
    n = y.shape[0]
    nb = tm // MOBA_BLOCK
    tiles = seq // tm
    full = lambda shape: pl.BlockSpec(shape, lambda b, t: (0,) * len(shape))
    row = lambda b, t: (b * tiles + t, 0)
    blk = lambda b, t: (b * tiles + t, 0, 0)
    tcol = lambda b, t: (b, 0, t)
    nblocks = n // MOBA_BLOCK
    return pl.pallas_call(
        functools.partial(_proj_fox_body, nb=nb),
        grid=(batch, tiles),
        in_specs=[pl.BlockSpec((tm, D_MODEL), row), full((D_MODEL, D_MODEL)), full((D_MODEL, D_MODEL)),
                  full((D_MODEL, D_MODEL)), full((D_MODEL, D_MODEL)), full((D_MODEL, LANES)),
                  full((1, LANES)), full((tm, tm)), full((LANES, D_MODEL))],
        out_specs=[
            pl.BlockSpec((1, D_MODEL, tm), tcol),
            pl.BlockSpec((1, D_MODEL, tm), tcol),
            pl.BlockSpec((nb, MOBA_BLOCK, D_MODEL), blk),
            pl.BlockSpec((nb, D_MODEL, MOBA_BLOCK), blk),
            pl.BlockSpec((nb, D_MODEL, MOBA_BLOCK), blk),
            pl.BlockSpec((tm, N_HEADS), row),
            pl.BlockSpec((nb, MOBA_BLOCK, D_MODEL), blk),
        ],
        out_shape=[
            jax.ShapeDtypeStruct((batch, D_MODEL, seq), F32),
            jax.ShapeDtypeStruct((batch, D_MODEL, seq), F32),
            jax.ShapeDtypeStruct((nblocks, MOBA_BLOCK, D_MODEL), BF16),
            jax.ShapeDtypeStruct((nblocks, D_MODEL, MOBA_BLOCK), BF16),
            jax.ShapeDtypeStruct((nblocks, D_MODEL, MOBA_BLOCK), BF16),
            jax.ShapeDtypeStruct((n, N_HEADS), F32),
            jax.ShapeDtypeStruct((nblocks, MOBA_BLOCK, D_MODEL), BF16),
        ],
        scratch_shapes=[pltpu.VMEM((8, LANES), F32)],
        compiler_params=_params("arbitrary", "arbitrary"),
        name="proj_fox",
    )(y, wk, wkt, wqt, wvt, wf, bf, tri, place)


def _proj_nat_body(y_ref, wq_ref, wk_ref, wv_ref, *rest, with_logf):
    if with_logf:
        wf_ref, bf_ref, q_ref, k_ref, v_ref, lf_ref = rest
    else:
        q_ref, k_ref, v_ref = rest
    yb = y_ref[...].astype(BF16)
    q_ref[...] = _dot(yb, wq_ref[...])
    k_ref[...] = _dot(yb, wk_ref[...])
    v_ref[...] = _dot(yb, wv_ref[...])
    if with_logf:
        lf_ref[...] = _logf_from(yb, wf_ref, bf_ref, y_ref.shape[0])[:, :N_HEADS]


def _proj_nat(y, wq, wk, wv, wf=None, bf=None):
    n = y.shape[0]
    with_logf = wf is not None
    full = lambda shape: pl.BlockSpec(shape, lambda i: (0,) * len(shape))
    ins = [y, wq, wk, wv]
    in_specs = [full((n, D_MODEL))] + [full((D_MODEL, D_MODEL))] * 3
    out_specs = [full((n, D_MODEL))] * 3
    out_shape = [jax.ShapeDtypeStruct((n, D_MODEL), F32)] * 3
    if with_logf:
        ins += [wf, bf]
        in_specs += [full((D_MODEL, LANES)), full((1, LANES))]
        out_specs += [full((n, N_HEADS))]
        out_shape += [jax.ShapeDtypeStruct((n, N_HEADS), F32)]
    return pl.pallas_call(
        functools.partial(_proj_nat_body, with_logf=with_logf),
        grid=(1,),
        in_specs=in_specs,
        out_specs=out_specs,
        out_shape=out_shape,
        compiler_params=_params("arbitrary"),
        name="proj_nat",
    )(*ins)


def _attn_body(kb_ref, aux_ref, qt_ref, vt_ref, qa_ref, *rest, n_blk, use_sel, sb):
    if use_sel:
        sel_ref, o_ref, s_ref = rest
    else:
        o_ref, s_ref = rest
    hp = pl.program_id(1)
    tile = sb * MOBA_BLOCK
    key_i = lax.broadcasted_iota(jnp.int32, (tile, tile), 0)
    qry_i = lax.broadcasted_iota(jnp.int32, (tile, tile), 1)
    causal = key_i <= qry_i
    rid = lax.broadcasted_iota(jnp.int32, (PAIR, tile), 0)
    hrows = [slice(i * HEAD_DIM, (i + 1) * HEAD_DIM) for i in range(2)]

    def cat(parts, axis):
        return parts[0] if len(parts) == 1 else jnp.concatenate(parts, axis=axis)

    def keys_of(t):
        kb = cat([kb_ref[t * sb + a] for a in range(sb)], 0)
        aux = cat([aux_ref[t * sb + a] for a in range(sb)], 0)
        return jnp.concatenate([kb, aux], axis=1)

    def vt_of(t, i):
        return cat([vt_ref[t * sb + a, hrows[i], :] for a in range(sb)], 1)

    def sel_rows(jt, kt, i, a, first_c):
        row = (kt * sb + a) * N_HEADS + 2 * hp + i
        parts = [sel_ref[jt * sb + c, pl.ds(row, 1), :] if c >= first_c
                 else jnp.zeros((1, MOBA_BLOCK), F32) for c in range(sb)]
        return cat(parts, 1)

    def add_sel(s, jt, kt, i, diag):
        if not use_sel:
            return s
        rows = []
        for a in range(sb):
            sa = s[a * MOBA_BLOCK:(a + 1) * MOBA_BLOCK, :]
            if diag and a == sb - 1:
                rows.append(sa)
            else:
                rows.append(sa + sel_rows(jt, kt, i, a, a + 1 if diag else 0))
        return cat(rows, 0)

    def q_tile(jt, carry):
        qt = cat([qt_ref[jt * sb + c] for c in range(sb)], 1)
        qaug = []
        for i in range(2):
            own = (rid >= i * HEAD_DIM) & (rid < (i + 1) * HEAD_DIM)
            qa = cat([qa_ref[i]] * sb, 1)
            qaug.append(jnp.concatenate([jnp.where(own, qt, jnp.zeros_like(qt)), qa], axis=0))

        def produce(kt, slot):
            kcat = keys_of(kt)
            for i in range(2):
                s_ref[slot, i] = _dot(kcat, qaug[i])

        def consume(c, kt, slot, diag):
            new = []
            for i in range(2):
                m, l, acc = c[3 * i:3 * i + 3]
                s = add_sel(s_ref[slot, i], jt, kt, i, diag)
                if diag:
                    s = jnp.where(causal, s, NEG)
                m_new = jnp.maximum(m, jnp.max(s, axis=0, keepdims=True))
                corr = jnp.exp(m - m_new)
                p = jnp.exp(s - m_new)
                l = l * corr + jnp.sum(p, axis=0, keepdims=True)
                acc = acc * corr + _dot(vt_of(kt, i), p.astype(BF16))
                new += [m_new, l, acc]
            return tuple(new)

        produce(0, 0)
        state = []
        for i in range(2):
            state += [jnp.full((1, tile), NEG, F32), jnp.zeros((1, tile), F32), jnp.zeros((HEAD_DIM, tile), F32)]

        def past(kt, c):
            slot = kt % 2
            c = consume(c, kt, slot, False)
            produce(kt + 1, 1 - slot)
            return c

        state = lax.fori_loop(0, jt, past, tuple(state))
        state = consume(state, jt, jt % 2, True)
        for i in range(2):
            m, l, acc = state[3 * i:3 * i + 3]
            out = (acc / l).astype(BF16)
            for c in range(sb):
                o_ref[jt * sb + c, hrows[i], :] = out[:, c * MOBA_BLOCK:(c + 1) * MOBA_BLOCK]
        return carry

    lax.fori_loop(0, n_blk // sb, q_tile, 0)


def _attn(kb, aux, qt, vt, qa, sel, batch, seq, aux_is_const):
    n_blk = seq // MOBA_BLOCK
    nblocks = kb.shape[0]
    use_sel = sel is not None
    aux_map = (lambda b, hp: (0, 0, 0)) if aux_is_const else (lambda b, hp: (b, 0, hp))
    in_specs = [
        pl.BlockSpec((n_blk, MOBA_BLOCK, PAIR), lambda b, hp: (b, 0, hp)),
        pl.BlockSpec((n_blk, MOBA_BLOCK, PAIR), aux_map),
        pl.BlockSpec((n_blk, PAIR, MOBA_BLOCK), lambda b, hp: (b, hp, 0)),
        pl.BlockSpec((n_blk, PAIR, MOBA_BLOCK), lambda b, hp: (b, hp, 0)),
        pl.BlockSpec((2, PAIR, MOBA_BLOCK), lambda b, hp: (hp, 0, 0)),
    ]
    ins = [kb, aux, qt, vt, qa]
    if use_sel:
        in_specs.append(pl.BlockSpec((n_blk, n_blk * N_HEADS, MOBA_BLOCK), lambda b, hp: (b, 0, 0)))
        ins.append(sel)
    return pl.pallas_call(
        functools.partial(_attn_body, n_blk=n_blk, use_sel=use_sel, sb=ATTN_BLOCKS_PER_TILE),
        grid=(batch, N_PAIRS),
        in_specs=in_specs,
        out_specs=pl.BlockSpec((n_blk, PAIR, MOBA_BLOCK), lambda b, hp: (b, hp, 0)),
        out_shape=jax.ShapeDtypeStruct((nblocks, D_MODEL, MOBA_BLOCK), BF16),
        scratch_shapes=[pltpu.VMEM((2, 2, ATTN_BLOCKS_PER_TILE * MOBA_BLOCK,
                                    ATTN_BLOCKS_PER_TILE * MOBA_BLOCK), F32)],
        compiler_params=_params("parallel", "parallel"),
        name="attn_moba" if use_sel else "attn_fox",
    )(*ins)


def _diag_rows(x, t):
    return jnp.sum(x.reshape(N_HEADS, t, D_MODEL), axis=0)


def _dot_exact_rhs(x, b_bf):
    hi, mid, lo = _split3(x)
    return _dot(hi, b_bf) + _dot(mid, b_bf) + _dot(lo, b_bf)


def _q_rows(q_ref, dm, scale):
    return jnp.concatenate([q_ref[...]] * N_HEADS, axis=0) * (dm * scale)


def _expand(c, dm, t):
    return _diag_rows(jnp.broadcast_to(c, dm.shape) * dm, t)


def _new_token_scores(qbt, kn_ref, vn_ref, nbias, extra_cols, t_new):
    pad = jnp.zeros((LANES - t_new, D_MODEL), F32)
    kn = jnp.concatenate([kn_ref[...], pad], axis=0).astype(BF16)
    vn = jnp.concatenate([vn_ref[...], pad], axis=0).astype(BF16)
    if extra_cols is not None:
        kn = jnp.concatenate([kn, extra_cols], axis=1)
    return _dot_nt(qbt, kn) + nbias, vn


def _moba_sample_body(pt_ref, q_ref, kn_ref, vn_ref, dm_ref, slb_ref, qp_ref, nbias_ref,
                      *rest, pp, t_new, n_blk):
    del pt_ref
    k_refs, v_refs = rest[:pp], rest[pp:2 * pp]
    o_ref, qbt_ref, qmid_ref, g_ref, m_ref, l_ref, od_ref = rest[2 * pp:]
    s = pl.program_id(1)
    rows = N_HEADS * t_new
    dm = dm_ref[...]
    lane = lax.broadcasted_iota(jnp.int32, (rows, LANES), 1)

    @pl.when(s == 0)
    def _():
        q16 = _q_rows(q_ref, dm, QK_SCALE)
        hi = q16.astype(BF16)
        qbt_ref[...] = hi
        qmid_ref[...] = (q16 - hi.astype(F32)).astype(BF16)
        g_ref[...] = jnp.zeros_like(g_ref)
        m_ref[...] = jnp.zeros_like(m_ref)
        l_ref[...] = jnp.zeros_like(l_ref)

    qbt = qbt_ref[...]
    per_blk = MOBA_BLOCK // PAGE_SIZE
    pair_rows = 2 * t_new
    first_head = lax.broadcasted_iota(jnp.int32, (t_new, LANES), 1) < HEAD_DIM

    def pair_q(ref, g):
        return ref[g * pair_rows:(g + 1) * pair_rows, g * PAIR:(g + 1) * PAIR]

    g_new = m_new = l_new = jnp.zeros((rows, LANES), F32)
    for i in range(pp // per_blk):
        n = s * (pp // per_blk) + i
        kf = jnp.concatenate([k_refs[per_blk * i + j][...] for j in range(per_blk)], axis=1)
        kt = kf.astype(BF16)
        vt = jnp.concatenate([v_refs[per_blk * i + j][...] for j in range(per_blk)], axis=1).astype(BF16)
        ks = kf[:, :LANES] + kf[:, LANES:]
        ks_hi = ks.astype(BF16)
        ks_mid = (ks - ks_hi.astype(F32)).astype(BF16)
        sc, gsum = [], []
        for g in range(N_PAIRS):
            feat = slice(g * PAIR, (g + 1) * PAIR)
            qg, qg_mid = pair_q(qbt_ref, g), pair_q(qmid_ref, g)
            sc.append(_dot(qg, kt[feat, :]))
            gsum.append(jnp.sum(_dot(qg, ks_hi[feat, :]), axis=1, keepdims=True)
                        + jnp.sum(_dot(qg_mid, ks_hi[feat, :]), axis=1, keepdims=True)
                        + jnp.sum(_dot(qg, ks_mid[feat, :]), axis=1, keepdims=True))
        sc = jnp.concatenate(sc, axis=0)
        gsum = jnp.concatenate(gsum, axis=0)
        dist = qp_ref[...] - (n * MOBA_BLOCK).astype(F32)
        sc = sc - slb_ref[...] * dist
        m = jnp.max(sc, axis=1, keepdims=True)
        e = jnp.exp(sc - m)
        l = jnp.sum(e, axis=1, keepdims=True)
        eb = e.astype(BF16)
        od = []
        for g in range(N_PAIRS):
            r = _dot_nt(eb[g * pair_rows:(g + 1) * pair_rows, :], vt[g * PAIR:(g + 1) * PAIR, :])
            od.append(jnp.where(first_head, r[:t_new, :], r[t_new:, :]))
        od_ref[n] = jnp.concatenate(od, axis=1)
        onehot = (lane == n).astype(F32)
        g_new = g_new + gsum * onehot
        m_new = m_new + m * onehot
        l_new = l_new + l * onehot
    g_ref[...] += g_new
    m_ref[...] += m_new
    l_ref[...] += l_new

    @pl.when(s == pl.num_programs(1) - 1)
    def _():
        low = jnp.float32(-3.0e38)
        g = jnp.where(lane < n_blk, g_ref[...], low)
        lane_f = lane.astype(F32)
        sel = lane < 0
        for _ in range(MOBA_TOPK):
            mx = jnp.max(g, axis=1, keepdims=True)
            first = jnp.min(jnp.where(g == mx, lane_f, float(LANES)), axis=1, keepdims=True)
            pick = lane_f == first
            sel = sel | pick
            g = jnp.where(pick, low, g)
        sn, vn = _new_token_scores(qbt, kn_ref, vn_ref, nbias_ref[...], None, t_new)
        m_n = jnp.max(sn, axis=1, keepdims=True)
        e_n = jnp.exp(sn - m_n)
        l_n = jnp.sum(e_n, axis=1, keepdims=True)
        od_n = _diag_rows(_dot(e_n.astype(BF16), vn) * dm, t_new)
        m_all = m_ref[...]
        m_tot = jnp.maximum(jnp.max(jnp.where(sel, m_all, NEG), axis=1, keepdims=True), m_n)
        w = jnp.where(sel, jnp.exp(jnp.minimum(m_all - m_tot, 0.0)), 0.0)
        w_n = jnp.exp(m_n - m_tot)
        l_tot = jnp.sum(w * l_ref[...], axis=1, keepdims=True) + w_n * l_n
        acc = _expand(w_n, dm, t_new) * od_n
        for n in range(n_blk):
            acc = acc + _expand(w[:, n:n + 1], dm, t_new) * od_ref[n]
        o_ref[...] = acc / _expand(l_tot, dm, t_new)


def _page_spec(i, pp, n_pages, reverse):
    def index(b, s, pt):
        p = s * pp + i
        if reverse:
            p = n_pages - 1 - p
        return (pt[b * n_pages + p], 0, 0)
    return index


def _moba_sample(q, k_new, v_new, cache_kt, cache_vt, page_table, consts, t_new):
    batch = q.shape[0] // t_new
    n_pages = page_table.shape[1]
    n_blk = n_pages * PAGE_SIZE // MOBA_BLOCK
    pp = PAGES_PER_STEP
    rows = N_HEADS * t_new
    tok = pl.BlockSpec((t_new, D_MODEL), lambda b, s, pt: (b, 0))
    const = lambda a: pl.BlockSpec(a.shape, lambda b, s, pt: (0, 0))
    page = lambda i: pl.BlockSpec((None, D_MODEL, PAGE_SIZE), _page_spec(i, pp, n_pages, False))
    grid_spec = pltpu.PrefetchScalarGridSpec(
        num_scalar_prefetch=1,
        grid=(batch, n_pages // pp),
        in_specs=[tok, tok, tok] + [const(a) for a in consts]
        + [page(i) for i in range(pp)] + [page(i) for i in range(pp)],
        out_specs=tok,
        scratch_shapes=[
            pltpu.VMEM((rows, D_MODEL), BF16),
            pltpu.VMEM((rows, D_MODEL), BF16),
            pltpu.VMEM((rows, LANES), F32),
            pltpu.VMEM((rows, LANES), F32),
            pltpu.VMEM((rows, LANES), F32),
            pltpu.VMEM((n_blk, t_new, D_MODEL), F32),
        ],
    )
    return pl.pallas_call(
        functools.partial(_moba_sample_body, pp=pp, t_new=t_new, n_blk=n_blk),
        grid_spec=grid_spec,
        out_shape=jax.ShapeDtypeStruct((batch * t_new, D_MODEL), F32),
        compiler_params=_params("parallel", "arbitrary"),
        name="moba_sample",
    )(page_table.reshape(-1), q, k_new, v_new, *consts,
      *([cache_kt] * pp), *([cache_vt] * pp))


def _fox_sample_body(pt_ref, q_ref, kn_ref, vn_ref, lfn_ref, dm_ref, qaux_ref, tri_ref, nbias_ref,
                     later_ref, ones_ref, *rest, pp, t_new):
    del pt_ref
    k_refs, v_refs, lf_refs = rest[:pp], rest[pp:2 * pp], rest[2 * pp:3 * pp]
    o_ref, qbt_ref, pad_ref, carry_ref, m_ref, l_ref, acc_ref = rest[3 * pp:]
    s = pl.program_id(1)
    dm = dm_ref[...]

    @pl.when(s == 0)
    def _():
        qbt_ref[:, :D_MODEL] = _q_rows(q_ref, dm, QK_SCALE).astype(BF16)
        qbt_ref[:, D_MODEL:] = qaux_ref[...]
        carry_ref[...] = jnp.zeros_like(carry_ref)
        m_ref[...] = jnp.full_like(m_ref, NEG)
        l_ref[...] = jnp.zeros_like(l_ref)
        acc_ref[...] = jnp.zeros_like(acc_ref)

    qbt = qbt_ref[...]

    def online(sc, weigh):
        m_old = m_ref[:, 0:1]
        m_new = jnp.maximum(m_old, jnp.max(sc, axis=1, keepdims=True))
        corr = jnp.exp(m_old - m_new)
        e = jnp.exp(sc - m_new)
        l_ref[...] = jnp.broadcast_to(l_ref[:, 0:1] * corr + jnp.sum(e, axis=1, keepdims=True), l_ref.shape)
        acc_ref[...] = acc_ref[...] * corr + weigh(e.astype(BF16))
        m_ref[...] = jnp.broadcast_to(m_new, m_ref.shape)

    zrows = jnp.zeros((LANES - 3 * N_HEADS, PAGE_SIZE), BF16)
    carry = carry_ref[...]
    kcols = []
    for i in range(pp):
        lft = lf_refs[i][...]
        suf = _dot_exact_rhs(lft, later_ref[...]) + carry
        carry = carry + _dot_exact_rhs(lft, ones_ref[...])
        hi, mid, lo = _split3(suf)
        kcols.append(jnp.concatenate([k_refs[i][...].astype(BF16), hi, mid, lo, zrows], axis=0))
    carry_ref[...] = carry
    kcat = jnp.concatenate(kcols, axis=1)
    vt = jnp.concatenate([v_refs[i][...].astype(BF16) for i in range(pp)], axis=1)
    online(_dot(qbt, kcat), lambda e: _dot_nt(e, vt))

    @pl.when(s == pl.num_programs(1) - 1)
    def _():
        pad_ref[...] = jnp.zeros_like(pad_ref)
        pad_ref[0:t_new, :N_HEADS] = lfn_ref[...]
        c_new = _dot_exact_lhs(tri_ref[...], pad_ref[...])
        sn, vn = _new_token_scores(qbt, kn_ref, vn_ref, nbias_ref[...], _pack3(-c_new), t_new)
        online(sn, lambda e: _dot(e, vn))
        out = acc_ref[...] / l_ref[:, 0:1]
        o_ref[...] = _diag_rows(out * dm, t_new)


def _fox_sample(q, k_new, v_new, lf_new, cache_kt, cache_vt, cache_lft, page_table, consts, t_new):
    batch = q.shape[0] // t_new
    n_pages = page_table.shape[1]
    pp = PAGES_PER_STEP
    rows = N_HEADS * t_new
    tok = pl.BlockSpec((t_new, D_MODEL), lambda b, s, pt: (b, 0))
    const = lambda a: pl.BlockSpec(a.shape, lambda b, s, pt: (0, 0))
    page = lambda i: pl.BlockSpec((None, D_MODEL, PAGE_SIZE), _page_spec(i, pp, n_pages, True))
    lfpage = lambda i: pl.BlockSpec((None, N_HEADS, PAGE_SIZE), _page_spec(i, pp, n_pages, True))
    grid_spec = pltpu.PrefetchScalarGridSpec(
        num_scalar_prefetch=1,
        grid=(batch, n_pages // pp),
        in_specs=[tok, tok, tok, pl.BlockSpec((t_new, N_HEADS), lambda b, s, pt: (b, 0))]
        + [const(a) for a in consts]
        + [page(i) for i in range(pp)] + [page(i) for i in range(pp)] + [lfpage(i) for i in range(pp)],
        out_specs=tok,
        scratch_shapes=[
            pltpu.VMEM((rows, D_MODEL + LANES), BF16),
            pltpu.VMEM((LANES, LANES), F32),
            pltpu.VMEM((N_HEADS, PAGE_SIZE), F32),
            pltpu.VMEM((rows, LANES), F32),
            pltpu.VMEM((rows, LANES), F32),
            pltpu.VMEM((rows, D_MODEL), F32),
        ],
    )
    return pl.pallas_call(
        functools.partial(_fox_sample_body, pp=pp, t_new=t_new),
        grid_spec=grid_spec,
        out_shape=jax.ShapeDtypeStruct((batch * t_new, D_MODEL), F32),
        compiler_params=_params("parallel", "arbitrary"),
        name="fox_sample",
    )(page_table.reshape(-1), q, k_new, v_new, lf_new, *consts,
      *([cache_kt] * pp), *([cache_vt] * pp), *([cache_lft] * pp))


def _np_split3(x):
    x = jnp.asarray(x, F32)
    hi, mid, lo = _split3(x)
    return hi, mid, lo


def _alibi_slopes():
    return 2.0 ** (-8.0 * np.arange(1, N_HEADS + 1) / N_HEADS)


def _prompt_tables(seq, tm):
    n_blk = seq // MOBA_BLOCK
    pos = np.zeros((n_blk, MOBA_BLOCK, PAIR), np.float32)
    pos[:, :, 0:3] = np.arange(MOBA_BLOCK, dtype=np.float32)[None, :, None]
    pos[:, :, 3:6] = (np.arange(n_blk, dtype=np.float32) * MOBA_BLOCK)[:, None, None]
    s_hi, s_mid, s_lo = _np_split3(_alibi_slopes())
    parts = jnp.stack([s_hi, s_mid, s_lo, s_hi, s_mid, s_lo], axis=1).astype(F32)
    qa_moba = jnp.zeros((N_HEADS, PAIR, MOBA_BLOCK), F32)
    qa_moba = qa_moba.at[:, 0:6, :].set(jnp.broadcast_to(parts[:, :, None], (N_HEADS, 6, MOBA_BLOCK)))
    qa_fox = np.zeros((N_HEADS, PAIR, MOBA_BLOCK), np.float32)
    for h in range(N_HEADS):
        i = h % 2
        qa_fox[h, 3 * i:3 * i + 3, :] = -1.0
    place = np.zeros((LANES, D_MODEL), np.float32)
    for h in range(N_HEADS):
        for piece in range(3):
            place[piece * N_HEADS + h, (h // 2) * PAIR + 3 * (h % 2) + piece] = 1.0
    tri = np.tril(np.ones((tm, tm), np.float32))
    hm = (np.arange(D_MODEL)[None, :] // HEAD_DIM == np.arange(N_HEADS)[:, None]).astype(np.float32)
    return dict(pos=jnp.asarray(pos, BF16), qa_moba=qa_moba.astype(BF16), qa_fox=jnp.asarray(qa_fox, BF16),
                place=jnp.asarray(place, BF16), tri=jnp.asarray(tri, BF16), hm=jnp.asarray(hm, F32))


def _sample_tables(t_new, past_len):
    rows = N_HEADS * t_new
    h_of = np.arange(rows) // t_new
    t_of = np.arange(rows) % t_new
    dm = (np.arange(D_MODEL)[None, :] // HEAD_DIM == h_of[:, None]).astype(np.float32)
    slopes = _alibi_slopes().astype(np.float32)
    slb = np.broadcast_to(slopes[h_of][:, None], (rows, MOBA_BLOCK)).astype(np.float32)
    qp = (past_len + t_of[:, None] - np.arange(MOBA_BLOCK)[None, :]).astype(np.float32)
    s_idx = np.arange(LANES)[None, :]
    visible = (s_idx <= t_of[:, None]) & (s_idx < t_new)
    nb_moba = np.where(visible, -slopes[h_of][:, None] * (t_of[:, None] - s_idx), NEG).astype(np.float32)
    nb_fox = np.where(visible, 0.0, NEG).astype(np.float32)
    qaux = np.zeros((rows, LANES), np.float32)
    for piece in range(3):
        qaux[np.arange(rows), piece * N_HEADS + h_of] = 1.0
    tri = np.tril(np.ones((LANES, LANES), np.float32))
    later = np.tril(np.ones((PAGE_SIZE, PAGE_SIZE), np.float32), k=-1)
    ones = np.ones((PAGE_SIZE, PAGE_SIZE), np.float32)
    bf = lambda a: jnp.asarray(a, BF16)
    f32 = lambda a: jnp.asarray(a, F32)
    return dict(
        moba=(f32(dm), f32(slb), f32(qp), f32(nb_moba)),
        fox=(f32(dm), bf(qaux), bf(tri), f32(nb_fox), bf(later), bf(ones)))


def kernel(x_prompt, x_sample, cache_k_moba, cache_v_moba, cache_k_fox, cache_v_fox, cache_logf_fox,
           page_table, w_in_moba, w_out_moba, w_in_fox, b_f_fox, w_out_fox, w_ffn_up, w_ffn_down,
           ln_g, ln_b):
    batch, seq, d = x_prompt.shape
    dec_batch, t_new, _ = x_sample.shape
    n_pool = cache_k_moba.shape[1]
    past_len = page_table.shape[1] * PAGE_SIZE
    assert d == D_MODEL and seq % 512 == 0 and t_new == 8
    tm = 512
    pt = _prompt_tables(seq, tm)
    st = _sample_tables(t_new, past_len)

    yp = x_prompt.reshape(batch * seq, d)
    ys = x_sample.reshape(dec_batch * t_new, d)
    n_s = ys.shape[0]

    def norm(i, j):
        return ln_g[i, j].reshape(1, d), ln_b[i, j].reshape(1, d)

    def ffn_both(yp, ys, i, j):
        wu = w_ffn_up[i, j].astype(BF16)
        wd = w_ffn_down[i, j].astype(BF16)
        g, b = norm(i, 2 * j)
        return _ffn(yp, wu, wd, g, b, tm), _ffn(ys, wu, wd, g, b, n_s)

    def heads(z, lead):
        return z.reshape(lead + (N_HEADS, HEAD_DIM))[None]

    def heads_t(zt):
        return jnp.transpose(zt.reshape(batch, N_HEADS, HEAD_DIM, seq), (0, 3, 1, 2))[None]

    def pages_t(cache):
        return jnp.transpose(cache, (0, 2, 3, 1)).reshape(n_pool, d, PAGE_SIZE)

    outs = {}
    for i in range(DEPTH):
        li = i // 2
        yp, ys = ffn_both(yp, ys, i, 0)
        if i % 2 == 0:
            w = w_in_moba[li].astype(BF16)
            wq, wk, wv = w[:, :d], w[:, d:2 * d], w[:, 2 * d:3 * d]
            kp, vp, kb, qt, vt, sel = _proj_moba(yp, wk, wk.T, wq.T, wv.T, pt["hm"], batch, seq, tm)
            qs, ks, vs = _proj_nat(ys, wq, wk, wv)
            mt = _attn(kb, pt["pos"], qt, vt, pt["qa_moba"], sel, batch, seq, True)
            ms = _moba_sample(qs, ks, vs, pages_t(cache_k_moba[li]), pages_t(cache_v_moba[li]),
                              page_table, st["moba"], t_new)
            w_out = w_out_moba[li].astype(BF16)
            outs["kmp"], outs["vmp"] = heads_t(kp), heads_t(vp)
            outs["kms"], outs["vms"] = heads(ks, (dec_batch, t_new)), heads(vs, (dec_batch, t_new))
        else:
            w = w_in_fox[li].astype(BF16)
            wq, wk, wv = w[:, :d], w[:, d:2 * d], w[:, 2 * d:3 * d]
            wf = jnp.pad(w[:, 3 * d:], ((0, 0), (0, LANES - N_HEADS)))
            bf = jnp.pad(b_f_fox[li].astype(F32).reshape(1, N_HEADS), ((0, 0), (0, LANES - N_HEADS)))
            kp, vp, kb, qt, vt, lfp, ck = _proj_fox(yp, wk, wk.T, wq.T, wv.T, wf, bf, pt["tri"], pt["place"],
                                                   batch, seq, tm)
            qs, ks, vs, lfs = _proj_nat(ys, wq, wk, wv, wf, bf)
            mt = _attn(kb, ck, qt, vt, pt["qa_fox"], None, batch, seq, False)
            ms = _fox_sample(qs, ks, vs, lfs, pages_t(cache_k_fox[li]), pages_t(cache_v_fox[li]),
                             jnp.transpose(cache_logf_fox[li], (0, 2, 1)), page_table, st["fox"], t_new)
            w_out = w_out_fox[li].astype(BF16)
            outs["kfp"], outs["vfp"] = heads_t(kp), heads_t(vp)
            outs["lfp"] = lfp.reshape(batch, seq, N_HEADS)[None]
            outs["kfs"], outs["vfs"] = heads(ks, (dec_batch, t_new)), heads(vs, (dec_batch, t_new))
            outs["lfs"] = lfs.reshape(dec_batch, t_new, N_HEADS)[None]
        g, b = norm(i, 1)
        yp = _outproj_t(yp, mt, w_out, g, b, tm)
        ys = _outproj(ys, ms, w_out, g, b)
        yp, ys = ffn_both(yp, ys, i, 1)

    return (yp.reshape(batch, seq, d), ys.reshape(dec_batch, t_new, d),
            outs["kmp"], outs["vmp"], outs["kfp"], outs["vfp"], outs["lfp"],
            outs["kms"], outs["vms"], outs["kfs"], outs["vfs"], outs["lfs"])
```

```python
import functools

import numpy as np
import jax
import jax.numpy as jnp
from jax import lax
from jax.experimental import pallas as pl
from jax.experimental.pallas import tpu as pltpu

F32 = jnp.float32
BF16 = jnp.bfloat16

D_MODEL = 1024
N_HEADS = 16
HEAD_DIM = D_MODEL // N_HEADS
D_FF = 2816
DEPTH = 2
MOBA_BLOCK = 256
MOBA_TOPK = 3
PAGE_SIZE = 128
DN_ALPHA = (2.0 * DEPTH) ** 0.25
LN_EPS = 1e-5
NEG = -1e30
QK_SCALE = HEAD_DIM ** -0.5

LANES = 128
PAIR = 2 * HEAD_DIM
N_PAIRS = N_HEADS // 2
FF_CHUNK = 1408
PAGES_PER_STEP = 16
ATTN_BLOCKS_PER_TILE = 2
VMEM_LIMIT = 56 * 1024 * 1024

_NT = (((1,), (1,)), ((), ()))


def _dot(a, b):
    return jnp.dot(a, b, preferred_element_type=F32)


def _dot_nt(a, b):
    return lax.dot_general(a, b, _NT, preferred_element_type=F32)


def _split3(x):
    hi = x.astype(BF16)
    r1 = x - hi.astype(F32)
    mid = r1.astype(BF16)
    lo = (r1 - mid.astype(F32)).astype(BF16)
    return hi, mid, lo


def _dot_exact_lhs(a_bf, x):
    hi, mid, lo = _split3(x)
    return _dot(a_bf, hi) + _dot(a_bf, mid) + _dot(a_bf, lo)


def _layer_norm(z, g, b):
    mu = jnp.mean(z, axis=-1, keepdims=True)
    zc = z - mu
    var = jnp.mean(zc * zc, axis=-1, keepdims=True)
    return zc * lax.rsqrt(var + LN_EPS) * g + b


def _log_sigmoid(x):
    return jnp.minimum(x, 0.0) - jnp.log1p(jnp.exp(-jnp.abs(x)))


def _params(*sem):
    return pltpu.CompilerParams(dimension_semantics=sem, vmem_limit_bytes=VMEM_LIMIT)


def _ffn_body(x_ref, wg_ref, wu_ref, wd_ref, g_ref, b_ref, o_ref, acc_ref, *, n_chunks):
    c = pl.program_id(1)
    x = x_ref[...]
    xb = x.astype(BF16)
    gate = _dot(xb, wg_ref[...])
    up = _dot(xb, wu_ref[...])
    h = (gate * jax.nn.sigmoid(gate)) * up
    part = _dot(h.astype(BF16), wd_ref[...])

    @pl.when(c == 0)
    def _():
        acc_ref[...] = part

    @pl.when(c > 0)
    def _():
        acc_ref[...] += part

    @pl.when(c == n_chunks - 1)
    def _():
        z = DN_ALPHA * x + 0.5 * acc_ref[...]
        o_ref[...] = _layer_norm(z, g_ref[...], b_ref[...])


def _ffn(x, w_up, w_down, layer, half, g, b, tm):
    n = x.shape[0]
    nc = D_FF // FF_CHUNK
    return pl.pallas_call(
        functools.partial(_ffn_body, n_chunks=nc),
        grid=(n // tm, nc),
        in_specs=[
            pl.BlockSpec((tm, D_MODEL), lambda i, c: (i, 0)),
            pl.BlockSpec((None, None, D_MODEL, FF_CHUNK), lambda i, c: (layer, half, 0, c)),
            pl.BlockSpec((None, None, D_MODEL, FF_CHUNK), lambda i, c: (layer, half, 0, c + D_FF // FF_CHUNK)),
            pl.BlockSpec((None, None, FF_CHUNK, D_MODEL), lambda i, c: (layer, half, c, 0)),
            pl.BlockSpec((1, D_MODEL), lambda i, c: (0, 0)),
            pl.BlockSpec((1, D_MODEL), lambda i, c: (0, 0)),
        ],
        out_specs=pl.BlockSpec((tm, D_MODEL), lambda i, c: (i, 0)),
        out_shape=jax.ShapeDtypeStruct((n, D_MODEL), F32),
        scratch_shapes=[pltpu.VMEM((tm, D_MODEL), F32)],
        compiler_params=_params("parallel", "arbitrary"),
        name="ffn",
    )(x, w_up, w_up, w_down, g, b)


def _outproj_t_body(y_ref, mt_ref, w_ref, g_ref, b_ref, o_ref, *, nb):
    for i in range(nb):
        rows = slice(i * MOBA_BLOCK, (i + 1) * MOBA_BLOCK)
        sub = lax.dot_general(mt_ref[i], w_ref[...], (((0,), (0,)), ((), ())),
                              preferred_element_type=F32)
        z = DN_ALPHA * y_ref[rows, :] + sub
        o_ref[rows, :] = _layer_norm(z, g_ref[...], b_ref[...])


def _outproj_t(y, mt, w_out, g, b, tm):
    n = y.shape[0]
    nb = tm // MOBA_BLOCK
    return pl.pallas_call(
        functools.partial(_outproj_t_body, nb=nb),
        grid=(n // tm,),
        in_specs=[
            pl.BlockSpec((tm, D_MODEL), lambda i: (i, 0)),
            pl.BlockSpec((nb, D_MODEL, MOBA_BLOCK), lambda i: (i, 0, 0)),
            pl.BlockSpec((D_MODEL, D_MODEL), lambda i: (0, 0)),
            pl.BlockSpec((1, D_MODEL), lambda i: (0, 0)),
            pl.BlockSpec((1, D_MODEL), lambda i: (0, 0)),
        ],
        out_specs=pl.BlockSpec((tm, D_MODEL), lambda i: (i, 0)),
        out_shape=jax.ShapeDtypeStruct((n, D_MODEL), F32),
        compiler_params=_params("parallel"),
        name="outproj_t",
    )(y, mt, w_out, g, b)


def _outproj_body(y_ref, m_ref, w_ref, g_ref, b_ref, o_ref):
    sub = _dot(m_ref[...].astype(BF16), w_ref[...])
    z = DN_ALPHA * y_ref[...] + sub
    o_ref[...] = _layer_norm(z, g_ref[...], b_ref[...])


def _outproj(y, m, w_out, g, b):
    n = y.shape[0]
    full = lambda shape: pl.BlockSpec(shape, lambda i: (0, 0))
    return pl.pallas_call(
        _outproj_body,
        grid=(1,),
        in_specs=[full((n, D_MODEL)), full((n, D_MODEL)), full((D_MODEL, D_MODEL)),
                  full((1, D_MODEL)), full((1, D_MODEL))],
        out_specs=full((n, D_MODEL)),
        out_shape=jax.ShapeDtypeStruct((n, D_MODEL), F32),
        compiler_params=_params("arbitrary"),
        name="outproj",
    )(y, m, w_out, g, b)


def _proj_common(y_ref, wk_ref, wkt_ref, wqt_ref, wvt_ref, k_ref, v_ref, kb_ref, qt_ref, vt_ref, nb):
    yb = y_ref[...].astype(BF16)
    k = _dot(yb, wk_ref[...])
    k_ref[0] = _dot_nt(wkt_ref[...], yb)
    kb = k.astype(BF16)
    qt = _dot_nt(wqt_ref[...], yb)
    vt = _dot_nt(wvt_ref[...], yb)
    v_ref[0] = vt
    for i in range(nb):
        cols = slice(i * MOBA_BLOCK, (i + 1) * MOBA_BLOCK)
        kb_ref[i] = kb[cols, :]
        qt_ref[i] = (qt[:, cols] * QK_SCALE).astype(BF16)
        vt_ref[i] = vt[:, cols].astype(BF16)
    return k, qt


def _proj_moba_body(y_ref, wk_ref, wkt_ref, wqt_ref, wvt_ref, hm_ref,
                    k_ref, v_ref, kb_ref, qt_ref, vt_ref, sel_ref, km_ref, *, nb, n_blk):
    t = pl.program_id(1)

    @pl.when(t == 0)
    def _():
        km_ref[...] = jnp.zeros_like(km_ref)

    k, qt = _proj_common(y_ref, wk_ref, wkt_ref, wqt_ref, wvt_ref,
                         k_ref, v_ref, kb_ref, qt_ref, vt_ref, nb)
    for i in range(nb):
        rows = slice(i * MOBA_BLOCK, (i + 1) * MOBA_BLOCK)
        km_ref[pl.ds(t * nb + i, 1), :] = jnp.sum(k[rows, :], axis=0, keepdims=True) / MOBA_BLOCK

    km = km_ref[...]
    hm = hm_ref[...]
    kmb = jnp.concatenate(
        [jnp.broadcast_to(km[n:n + 1, :], (N_HEADS, D_MODEL)) * hm for n in range(n_blk)], axis=0)
    a_hi = kmb.astype(BF16)
    a_mid = (kmb - a_hi.astype(F32)).astype(BF16)
    q_hi = qt.astype(BF16)
    q_mid = (qt - q_hi.astype(F32)).astype(BF16)
    gate = _dot(a_hi, q_hi) + _dot(a_hi, q_mid) + _dot(a_mid, q_hi)

    tm = nb * MOBA_BLOCK
    col = lax.broadcasted_iota(jnp.int32, (N_HEADS, tm), 1)
    q_blk = t * nb + col // MOBA_BLOCK
    g = []
    for n in range(n_blk):
        g.append(jnp.where(n < q_blk, gate[n * N_HEADS:(n + 1) * N_HEADS, :], NEG))
    bias = []
    for n in range(n_blk):
        rank = jnp.zeros((N_HEADS, tm), jnp.int32)
        for m in range(n_blk):
            if m == n:
                continue
            ahead = (g[m] >= g[n]) if m < n else (g[m] > g[n])
            rank = rank + ahead.astype(jnp.int32)
        keep = (n < q_blk) & (rank < MOBA_TOPK)
        bias.append(jnp.where(keep, 0.0, NEG))
    bias = jnp.concatenate(bias, axis=0)
    for i in range(nb):
        sel_ref[i] = bias[:, i * MOBA_BLOCK:(i + 1) * MOBA_BLOCK]


def _proj_moba(y, wk, wkt, wqt, wvt, hm, batch, seq, tm):
    n = y.shape[0]
    nb = tm // MOBA_BLOCK
    n_blk = seq // MOBA_BLOCK
    tiles = seq // tm
    full = lambda shape: pl.BlockSpec(shape, lambda b, t: (0,) * len(shape))
    row = lambda b, t: (b * tiles + t, 0)
    blk = lambda b, t: (b * tiles + t, 0, 0)
    tcol = lambda b, t: (b, 0, t)
    nblocks = n // MOBA_BLOCK
    return pl.pallas_call(
        functools.partial(_proj_moba_body, nb=nb, n_blk=n_blk),
        grid=(batch, tiles),
        in_specs=[pl.BlockSpec((tm, D_MODEL), row), full((D_MODEL, D_MODEL)), full((D_MODEL, D_MODEL)),
                  full((D_MODEL, D_MODEL)), full((D_MODEL, D_MODEL)), full((N_HEADS, D_MODEL))],
        out_specs=[
            pl.BlockSpec((1, D_MODEL, tm), tcol),
            pl.BlockSpec((1, D_MODEL, tm), tcol),
            pl.BlockSpec((nb, MOBA_BLOCK, D_MODEL), blk),
            pl.BlockSpec((nb, D_MODEL, MOBA_BLOCK), blk),
            pl.BlockSpec((nb, D_MODEL, MOBA_BLOCK), blk),
            pl.BlockSpec((nb, n_blk * N_HEADS, MOBA_BLOCK), blk),
        ],
        out_shape=[
            jax.ShapeDtypeStruct((batch, D_MODEL, seq), F32),
            jax.ShapeDtypeStruct((batch, D_MODEL, seq), F32),
            jax.ShapeDtypeStruct((nblocks, MOBA_BLOCK, D_MODEL), BF16),
            jax.ShapeDtypeStruct((nblocks, D_MODEL, MOBA_BLOCK), BF16),
            jax.ShapeDtypeStruct((nblocks, D_MODEL, MOBA_BLOCK), BF16),
            jax.ShapeDtypeStruct((nblocks, n_blk * N_HEADS, MOBA_BLOCK), F32),
        ],
        scratch_shapes=[pltpu.VMEM((n_blk, D_MODEL), F32)],
        compiler_params=_params("arbitrary", "arbitrary"),
        name="proj_moba",
    )(y, wk, wkt, wqt, wvt, hm)


def _logf_from(yb, wf_ref, bf_ref, tm):
    lane = lax.broadcasted_iota(jnp.int32, (tm, LANES), 1)
    zf = _dot(yb, wf_ref[...]) + bf_ref[...]
    return jnp.where(lane < N_HEADS, _log_sigmoid(zf), 0.0)


def _pack3(x):
    hi, mid, lo = _split3(x)
    packed = (hi.astype(F32) + pltpu.roll(mid.astype(F32), N_HEADS, 1)
              + pltpu.roll(lo.astype(F32), 2 * N_HEADS, 1))
    return packed.astype(BF16)


def _proj_fox_body(y_ref, wk_ref, wkt_ref, wqt_ref, wvt_ref, wf_ref, bf_ref, tri_ref, place_ref,
                   k_ref, v_ref, kb_ref, qt_ref, vt_ref, lf_ref, ck_ref, carry_ref, *, nb):
    t = pl.program_id(1)

    @pl.when(t == 0)
    def _():
        carry_ref[...] = jnp.zeros_like(carry_ref)

    _proj_common(y_ref, wk_ref, wkt_ref, wqt_ref, wvt_ref, k_ref, v_ref, kb_ref, qt_ref, vt_ref, nb)
    tm = nb * MOBA_BLOCK
    yb = y_ref[...].astype(BF16)
    lf = _logf_from(yb, wf_ref, bf_ref, tm)
    lf_ref[...] = lf[:, :N_HEADS]
    ct = _dot_exact_lhs(tri_ref[...], lf) + carry_ref[0:1, :]
    carry_ref[0:1, :] = ct[tm - 1:tm, :]
    ck = _dot(_pack3(ct), place_ref[...]).astype(BF16)
    for i in range(nb):
        ck_ref[i] = ck[i * MOBA_BLOCK:(i + 1) * MOBA_BLOCK, :]


def _proj_fox(y, wk, wkt, wqt, wvt, wf, bf, tri, place, batch, seq, tm):
    n = y.shape[0]
    nb = tm // MOBA_BLOCK
    tiles = seq // tm
    full = lambda shape: pl.BlockSpec(shape, lambda b, t: (0,) * len(shape))
    row = lambda b, t: (b * tiles + t, 0)
    blk = lambda b, t: (b * tiles + t, 0, 0)
    tcol = lambda b, t: (b, 0, t)
    nblocks = n // MOBA_BLOCK
    return pl.pallas_call(
        functools.partial(_proj_fox_body, nb=nb),
        grid=(batch, tiles),
        in_specs=[pl.BlockSpec((tm, D_MODEL), row), full((D_MODEL, D_MODEL)), full((D_MODEL, D_MODEL)),
                  full((D_MODEL, D_MODEL)), full((D_MODEL, D_MODEL)), full((D_MODEL, LANES)),
                  full((1, LANES)), full((tm, tm)), full((LANES, D_MODEL))],
        out_specs=[
            pl.BlockSpec((1, D_MODEL, tm), tcol),
            pl.BlockSpec((1, D_MODEL, tm), tcol),
            pl.BlockSpec((nb, MOBA_BLOCK, D_MODEL), blk),
            pl.BlockSpec((nb, D_MODEL, MOBA_BLOCK), blk),
            pl.BlockSpec((nb, D_MODEL, MOBA_BLOCK), blk),
            pl.BlockSpec((tm, N_HEADS), row),
            pl.BlockSpec((nb, MOBA_BLOCK, D_MODEL), blk),
        ],
        out_shape=[
            jax.ShapeDtypeStruct((batch, D_MODEL, seq), F32),
            jax.ShapeDtypeStruct((batch, D_MODEL, seq), F32),
            jax.ShapeDtypeStruct((nblocks, MOBA_BLOCK, D_MODEL), BF16),
            jax.ShapeDtypeStruct((nblocks, D_MODEL, MOBA_BLOCK), BF16),
            jax.ShapeDtypeStruct((nblocks, D_MODEL, MOBA_BLOCK), BF16),
            jax.ShapeDtypeStruct((n, N_HEADS), F32),
            jax.ShapeDtypeStruct((nblocks, MOBA_BLOCK, D_MODEL), BF16),
        ],
        scratch_shapes=[pltpu.VMEM((8, LANES), F32)],
        compiler_params=_params("arbitrary", "arbitrary"),
        name="proj_fox",
    )(y, wk, wkt, wqt, wvt, wf, bf, tri, place)


def _proj_nat_body(y_ref, wq_ref, wk_ref, wv_ref, *rest, with_logf):
    if with_logf:
        wf_ref, bf_ref, q_ref, k_ref, v_ref, lf_ref = rest
    else:
        q_ref, k_ref, v_ref = rest
    yb = y_ref[...].astype(BF16)
    q_ref[...] = _dot(yb, wq_ref[...])
    k_ref[...] = _dot(yb, wk_ref[...])
    v_ref[...] = _dot(yb, wv_ref[...])
    if with_logf:
        lf_ref[...] = _logf_from(yb, wf_ref, bf_ref, y_ref.shape[0])[:, :N_HEADS]


def _proj_nat(y, wq, wk, wv, wf=None, bf=None):
    n = y.shape[0]
    with_logf = wf is not None
    full = lambda shape: pl.BlockSpec(shape, lambda i: (0,) * len(shape))
    ins = [y, wq, wk, wv]
    in_specs = [full((n, D_MODEL))] + [full((D_MODEL, D_MODEL))] * 3
    out_specs = [full((n, D_MODEL))] * 3
    out_shape = [jax.ShapeDtypeStruct((n, D_MODEL), F32)] * 3
    if with_logf:
        ins += [wf, bf]
        in_specs += [full((D_MODEL, LANES)), full((1, LANES))]
        out_specs += [full((n, N_HEADS))]
        out_shape += [jax.ShapeDtypeStruct((n, N_HEADS), F32)]
    return pl.pallas_call(
        functools.partial(_proj_nat_body, with_logf=with_logf),
        grid=(1,),
        in_specs=in_specs,
        out_specs=out_specs,
        out_shape=out_shape,
        compiler_params=_params("arbitrary"),
        name="proj_nat",
    )(*ins)


def _attn_body(kb_ref, aux_ref, qt_ref, vt_ref, qa_ref, *rest, n_blk, use_sel, sb):
    if use_sel:
        sel_ref, o_ref, s_ref = rest
    else:
        o_ref, s_ref = rest
    hp = pl.program_id(1)
    tile = sb * MOBA_BLOCK
    key_i = lax.broadcasted_iota(jnp.int32, (tile, tile), 0)
    qry_i = lax.broadcasted_iota(jnp.int32, (tile, tile), 1)
    causal = key_i <= qry_i
    rid = lax.broadcasted_iota(jnp.int32, (PAIR, tile), 0)
    hrows = [slice(i * HEAD_DIM, (i + 1) * HEAD_DIM) for i in range(2)]

    def cat(parts, axis):
        return parts[0] if len(parts) == 1 else jnp.concatenate(parts, axis=axis)

    def keys_of(t):
        kb = cat([kb_ref[t * sb + a] for a in range(sb)], 0)
        aux = cat([aux_ref[t * sb + a] for a in range(sb)], 0)
        return jnp.concatenate([kb, aux], axis=1)

    def vt_of(t, i):
        return cat([vt_ref[t * sb + a, hrows[i], :] for a in range(sb)], 1)

    def sel_rows(jt, kt, i, a, first_c):
        row = (kt * sb + a) * N_HEADS + 2 * hp + i
        parts = [sel_ref[jt * sb + c, pl.ds(row, 1), :] if c >= first_c
                 else jnp.zeros((1, MOBA_BLOCK), F32) for c in range(sb)]
        return cat(parts, 1)

    def add_sel(s, jt, kt, i, diag):
        if not use_sel:
            return s
        rows = []
        for a in range(sb):
            sa = s[a * MOBA_BLOCK:(a + 1) * MOBA_BLOCK, :]
            if diag and a == sb - 1:
                rows.append(sa)
            else:
                rows.append(sa + sel_rows(jt, kt, i, a, a + 1 if diag else 0))
        return cat(rows, 0)

    def q_tile(jt, carry):
        qt = cat([qt_ref[jt * sb + c] for c in range(sb)], 1)
        qaug = []
        for i in range(2):
            own = (rid >= i * HEAD_DIM) & (rid < (i + 1) * HEAD_DIM)
            qa = cat([qa_ref[i]] * sb, 1)
            qaug.append(jnp.concatenate([jnp.where(own, qt, jnp.zeros_like(qt)), qa], axis=0))

        def produce(kt, slot):
            kcat = keys_of(kt)
            for i in range(2):
                s_ref[slot, i] = _dot(kcat, qaug[i])

        def consume(c, kt, slot, diag):
            new = []
            for i in range(2):
                m, l, acc = c[3 * i:3 * i + 3]
                s = add_sel(s_ref[slot, i], jt, kt, i, diag)
                if diag:
                    s = jnp.where(causal, s, NEG)
                m_new = jnp.maximum(m, jnp.max(s, axis=0, keepdims=True))
                corr = jnp.exp(m - m_new)
                p = jnp.exp(s - m_new)
                l = l * corr + jnp.sum(p, axis=0, keepdims=True)
                acc = acc * corr + _dot(vt_of(kt, i), p.astype(BF16))
                new += [m_new, l, acc]
            return tuple(new)

        produce(0, 0)
        state = []
        for i in range(2):
            state += [jnp.full((1, tile), NEG, F32), jnp.zeros((1, tile), F32), jnp.zeros((HEAD_DIM, tile), F32)]

        def past(kt, c):
            slot = kt % 2
            c = consume(c, kt, slot, False)
            produce(kt + 1, 1 - slot)
            return c

        state = lax.fori_loop(0, jt, past, tuple(state))
        state = consume(state, jt, jt % 2, True)
        for i in range(2):
            m, l, acc = state[3 * i:3 * i + 3]
            out = (acc / l).astype(BF16)
            for c in range(sb):
                o_ref[jt * sb + c, hrows[i], :] = out[:, c * MOBA_BLOCK:(c + 1) * MOBA_BLOCK]
        return carry

    lax.fori_loop(0, n_blk // sb, q_tile, 0)


def _attn(kb, aux, qt, vt, qa, sel, batch, seq, aux_is_const):
    n_blk = seq // MOBA_BLOCK
    nblocks = kb.shape[0]
    use_sel = sel is not None
    aux_map = (lambda b, hp: (0, 0, 0)) if aux_is_const else (lambda b, hp: (b, 0, hp))
    in_specs = [
        pl.BlockSpec((n_blk, MOBA_BLOCK, PAIR), lambda b, hp: (b, 0, hp)),
        pl.BlockSpec((n_blk, MOBA_BLOCK, PAIR), aux_map),
        pl.BlockSpec((n_blk, PAIR, MOBA_BLOCK), lambda b, hp: (b, hp, 0)),
        pl.BlockSpec((n_blk, PAIR, MOBA_BLOCK), lambda b, hp: (b, hp, 0)),
        pl.BlockSpec((2, PAIR, MOBA_BLOCK), lambda b, hp: (hp, 0, 0)),
    ]
    ins = [kb, aux, qt, vt, qa]
    if use_sel:
        in_specs.append(pl.BlockSpec((n_blk, n_blk * N_HEADS, MOBA_BLOCK), lambda b, hp: (b, 0, 0)))
        ins.append(sel)
    return pl.pallas_call(
        functools.partial(_attn_body, n_blk=n_blk, use_sel=use_sel, sb=ATTN_BLOCKS_PER_TILE),
        grid=(batch, N_PAIRS),
        in_specs=in_specs,
        out_specs=pl.BlockSpec((n_blk, PAIR, MOBA_BLOCK), lambda b, hp: (b, hp, 0)),
        out_shape=jax.ShapeDtypeStruct((nblocks, D_MODEL, MOBA_BLOCK), BF16),
        scratch_shapes=[pltpu.VMEM((2, 2, ATTN_BLOCKS_PER_TILE * MOBA_BLOCK,
                                    ATTN_BLOCKS_PER_TILE * MOBA_BLOCK), F32)],
        compiler_params=_params("parallel", "parallel"),
        name="attn_moba" if use_sel else "attn_fox",
    )(*ins)


def _diag_rows(x, t):
    return jnp.sum(x.reshape(N_HEADS, t, D_MODEL), axis=0)


def _dot_exact_rhs(x, b_bf):
    hi, mid, lo = _split3(x)
    return _dot(hi, b_bf) + _dot(mid, b_bf) + _dot(lo, b_bf)


def _q_rows(q_ref, dm, scale):
    return jnp.concatenate([q_ref[...]] * N_HEADS, axis=0) * (dm * scale)


def _new_token_scores(qbt, kn_ref, vn_ref, nbias, extra_cols, t_new):
    pad = jnp.zeros((LANES - t_new, D_MODEL), F32)
    kn = jnp.concatenate([kn_ref[...], pad], axis=0).astype(BF16)
    vn = jnp.concatenate([vn_ref[...], pad], axis=0).astype(BF16)
    if extra_cols is not None:
        kn = jnp.concatenate([kn, extra_cols], axis=1)
    return _dot_nt(qbt, kn) + nbias, vn


def _moba_sample_body(pt_ref, q_ref, kn_ref, vn_ref, dm_ref, slb_ref, qp_ref, nbias_ref,
                      *rest, pp, t_new, n_blk):
    del pt_ref
    k_refs, v_refs = rest[:pp], rest[pp:2 * pp]
    o_ref, qbt_ref, qmid_ref, g_ref, m_ref, l_ref, od_ref = rest[2 * pp:]
    s = pl.program_id(1)
    rows = N_HEADS * t_new
    dm = dm_ref[...]
    lane = lax.broadcasted_iota(jnp.int32, (rows, LANES), 1)

    @pl.when(s == 0)
    def _():
        q16 = _q_rows(q_ref, dm, QK_SCALE)
        hi = q16.astype(BF16)
        qbt_ref[...] = hi
        qmid_ref[...] = (q16 - hi.astype(F32)).astype(BF16)
        g_ref[...] = jnp.zeros_like(g_ref)
        m_ref[...] = jnp.zeros_like(m_ref)
        l_ref[...] = jnp.zeros_like(l_ref)

    qbt = qbt_ref[...]
    per_blk = MOBA_BLOCK // PAGE_SIZE
    pair_rows = 2 * t_new
    first_head = lax.broadcasted_iota(jnp.int32, (t_new, LANES), 1) < HEAD_DIM

    def pair_q(ref, g):
        return ref[g * pair_rows:(g + 1) * pair_rows, g * PAIR:(g + 1) * PAIR]

    g_new = m_new = l_new = jnp.zeros((rows, LANES), F32)
    for i in range(pp // per_blk):
        n = s * (pp // per_blk) + i
        kf = jnp.concatenate([k_refs[per_blk * i + j][...] for j in range(per_blk)], axis=1)
        kt = kf.astype(BF16)
        vt = jnp.concatenate([v_refs[per_blk * i + j][...] for j in range(per_blk)], axis=1).astype(BF16)
        ks = kf[:, :LANES] + kf[:, LANES:]
        ks_hi = ks.astype(BF16)
        ks_mid = (ks - ks_hi.astype(F32)).astype(BF16)
        sc, gsum = [], []
        for g in range(N_PAIRS):
            feat = slice(g * PAIR, (g + 1) * PAIR)
            qg, qg_mid = pair_q(qbt_ref, g), pair_q(qmid_ref, g)
            sc.append(_dot(qg, kt[feat, :]))
            gsum.append(jnp.sum(_dot(qg, ks_hi[feat, :]), axis=1, keepdims=True)
                        + jnp.sum(_dot(qg_mid, ks_hi[feat, :]), axis=1, keepdims=True)
                        + jnp.sum(_dot(qg, ks_mid[feat, :]), axis=1, keepdims=True))
        sc = jnp.concatenate(sc, axis=0)
        gsum = jnp.concatenate(gsum, axis=0)
        dist = qp_ref[...] - (n * MOBA_BLOCK).astype(F32)
        sc = sc - slb_ref[...] * dist
        m = jnp.max(sc, axis=1, keepdims=True)
        e = jnp.exp(sc - m)
        l = jnp.sum(e, axis=1, keepdims=True)
        eb = e.astype(BF16)
        od_ref[n] = jnp.concatenate(
            [_dot_nt(eb[g * pair_rows:(g + 1) * pair_rows, :], vt[g * PAIR:(g + 1) * PAIR, :])
             for g in range(N_PAIRS)], axis=0)
        onehot = (lane == n).astype(F32)
        g_new = g_new + gsum * onehot
        m_new = m_new + m * onehot
        l_new = l_new + l * onehot
    g_ref[...] += g_new
    m_ref[...] += m_new
    l_ref[...] += l_new

    @pl.when(s == pl.num_programs(1) - 1)
    def _():
        low = jnp.float32(-3.0e38)
        g = jnp.where(lane < n_blk, g_ref[...], low)
        lane_f = lane.astype(F32)
        sel = lane < 0
        for _ in range(MOBA_TOPK):
            mx = jnp.max(g, axis=1, keepdims=True)
            first = jnp.min(jnp.where(g == mx, lane_f, float(LANES)), axis=1, keepdims=True)
            pick = lane_f == first
            sel = sel | pick
            g = jnp.where(pick, low, g)
        sn, vn = _new_token_scores(qbt, kn_ref, vn_ref, nbias_ref[...], None, t_new)
        m_n = jnp.max(sn, axis=1, keepdims=True)
        e_n = jnp.exp(sn - m_n)
        l_n = jnp.sum(e_n, axis=1, keepdims=True)
        o_n = _dot(e_n.astype(BF16), vn)
        od_n = jnp.concatenate([o_n[g * pair_rows:(g + 1) * pair_rows, g * PAIR:(g + 1) * PAIR]
                                for g in range(N_PAIRS)], axis=0)
        m_all = m_ref[...]
        m_tot = jnp.maximum(jnp.max(jnp.where(sel, m_all, NEG), axis=1, keepdims=True), m_n)
        w = jnp.where(sel, jnp.exp(jnp.minimum(m_all - m_tot, 0.0)), 0.0)
        w_n = jnp.exp(m_n - m_tot)
        l_tot = jnp.sum(w * l_ref[...], axis=1, keepdims=True) + w_n * l_n
        acc = w_n * od_n
        for n in range(n_blk):
            acc = acc + w[:, n:n + 1] * od_ref[n]
        out = acc / l_tot
        o_ref[...] = jnp.concatenate(
            [jnp.where(first_head, out[g * pair_rows:g * pair_rows + t_new, :],
                       out[g * pair_rows + t_new:(g + 1) * pair_rows, :]) for g in range(N_PAIRS)], axis=1)


def _page_spec(i, pp, n_pages, reverse):
    def index(b, s, pt):
        p = s * pp + i
        if reverse:
            p = n_pages - 1 - p
        return (pt[b * n_pages + p], 0, 0)
    return index


def _moba_sample(q, k_new, v_new, cache_kt, cache_vt, page_table, consts, t_new):
    batch = q.shape[0] // t_new
    n_pages = page_table.shape[1]
    n_blk = n_pages * PAGE_SIZE // MOBA_BLOCK
    pp = PAGES_PER_STEP
    rows = N_HEADS * t_new
    tok = pl.BlockSpec((t_new, D_MODEL), lambda b, s, pt: (b, 0))
    const = lambda a: pl.BlockSpec(a.shape, lambda b, s, pt: (0, 0))
    page = lambda i: pl.BlockSpec((None, D_MODEL, PAGE_SIZE), _page_spec(i, pp, n_pages, False))
    grid_spec = pltpu.PrefetchScalarGridSpec(
        num_scalar_prefetch=1,
        grid=(batch, n_pages // pp),
        in_specs=[tok, tok, tok] + [const(a) for a in consts]
        + [page(i) for i in range(pp)] + [page(i) for i in range(pp)],
        out_specs=tok,
        scratch_shapes=[
            pltpu.VMEM((rows, D_MODEL), BF16),
            pltpu.VMEM((rows, D_MODEL), BF16),
            pltpu.VMEM((rows, LANES), F32),
            pltpu.VMEM((rows, LANES), F32),
            pltpu.VMEM((rows, LANES), F32),
            pltpu.VMEM((n_blk, rows, LANES), F32),
        ],
    )
    return pl.pallas_call(
        functools.partial(_moba_sample_body, pp=pp, t_new=t_new, n_blk=n_blk),
        grid_spec=grid_spec,
        out_shape=jax.ShapeDtypeStruct((batch * t_new, D_MODEL), F32),
        compiler_params=_params("parallel", "arbitrary"),
        name="moba_sample",
    )(page_table.reshape(-1), q, k_new, v_new, *consts,
      *([cache_kt] * pp), *([cache_vt] * pp))


def _fox_sample_body(pt_ref, q_ref, kn_ref, vn_ref, lfn_ref, dm_ref, qaux_ref, tri_ref, nbias_ref,
                     later_ref, ones_ref, *rest, pp, t_new):
    del pt_ref
    k_refs, v_refs, lf_refs = rest[:pp], rest[pp:2 * pp], rest[2 * pp:3 * pp]
    o_ref, qbt_ref, pad_ref, carry_ref, m_ref, l_ref, acc_ref = rest[3 * pp:]
    s = pl.program_id(1)
    dm = dm_ref[...]

    @pl.when(s == 0)
    def _():
        qbt_ref[:, :D_MODEL] = _q_rows(q_ref, dm, QK_SCALE).astype(BF16)
        qbt_ref[:, D_MODEL:] = qaux_ref[...]
        carry_ref[...] = jnp.zeros_like(carry_ref)
        m_ref[...] = jnp.full_like(m_ref, NEG)
        l_ref[...] = jnp.zeros_like(l_ref)
        acc_ref[...] = jnp.zeros_like(acc_ref)

    qbt = qbt_ref[...]

    def online(sc, weigh):
        m_old = m_ref[:, 0:1]
        m_new = jnp.maximum(m_old, jnp.max(sc, axis=1, keepdims=True))
        corr = jnp.exp(m_old - m_new)
        e = jnp.exp(sc - m_new)
        l_ref[...] = jnp.broadcast_to(l_ref[:, 0:1] * corr + jnp.sum(e, axis=1, keepdims=True), l_ref.shape)
        acc_ref[...] = acc_ref[...] * corr + weigh(e.astype(BF16))
        m_ref[...] = jnp.broadcast_to(m_new, m_ref.shape)

    zrows = jnp.zeros((LANES - 3 * N_HEADS, PAGE_SIZE), BF16)
    carry = carry_ref[...]
    kcols = []
    for i in range(pp):
        lft = lf_refs[i][...]
        suf = _dot_exact_rhs(lft, later_ref[...]) + carry
        carry = carry + _dot_exact_rhs(lft, ones_ref[...])
        hi, mid, lo = _split3(suf)
        kcols.append(jnp.concatenate([k_refs[i][...].astype(BF16), hi, mid, lo, zrows], axis=0))
    carry_ref[...] = carry
    kcat = jnp.concatenate(kcols, axis=1)
    vt = jnp.concatenate([v_refs[i][...].astype(BF16) for i in range(pp)], axis=1)
    online(_dot(qbt, kcat), lambda e: _dot_nt(e, vt))

    @pl.when(s == pl.num_programs(1) - 1)
    def _():
        pad_ref[...] = jnp.zeros_like(pad_ref)
        pad_ref[0:t_new, :N_HEADS] = lfn_ref[...]
        c_new = _dot_exact_lhs(tri_ref[...], pad_ref[...])
        sn, vn = _new_token_scores(qbt, kn_ref, vn_ref, nbias_ref[...], _pack3(-c_new), t_new)
        online(sn, lambda e: _dot(e, vn))
        out = acc_ref[...] / l_ref[:, 0:1]
        o_ref[...] = _diag_rows(out * dm, t_new)


def _fox_sample(q, k_new, v_new, lf_new, cache_kt, cache_vt, cache_lft, page_table, consts, t_new):
    batch = q.shape[0] // t_new
    n_pages = page_table.shape[1]
    pp = PAGES_PER_STEP
    rows = N_HEADS * t_new
    tok = pl.BlockSpec((t_new, D_MODEL), lambda b, s, pt: (b, 0))
    const = lambda a: pl.BlockSpec(a.shape, lambda b, s, pt: (0, 0))
    page = lambda i: pl.BlockSpec((None, D_MODEL, PAGE_SIZE), _page_spec(i, pp, n_pages, True))
    lfpage = lambda i: pl.BlockSpec((None, N_HEADS, PAGE_SIZE), _page_spec(i, pp, n_pages, True))
    grid_spec = pltpu.PrefetchScalarGridSpec(
        num_scalar_prefetch=1,
        grid=(batch, n_pages // pp),
        in_specs=[tok, tok, tok, pl.BlockSpec((t_new, N_HEADS), lambda b, s, pt: (b, 0))]
        + [const(a) for a in consts]
        + [page(i) for i in range(pp)] + [page(i) for i in range(pp)] + [lfpage(i) for i in range(pp)],
        out_specs=tok,
        scratch_shapes=[
            pltpu.VMEM((rows, D_MODEL + LANES), BF16),
            pltpu.VMEM((LANES, LANES), F32),
            pltpu.VMEM((N_HEADS, PAGE_SIZE), F32),
            pltpu.VMEM((rows, LANES), F32),
            pltpu.VMEM((rows, LANES), F32),
            pltpu.VMEM((rows, D_MODEL), F32),
        ],
    )
    return pl.pallas_call(
        functools.partial(_fox_sample_body, pp=pp, t_new=t_new),
        grid_spec=grid_spec,
        out_shape=jax.ShapeDtypeStruct((batch * t_new, D_MODEL), F32),
        compiler_params=_params("parallel", "arbitrary"),
        name="fox_sample",
    )(page_table.reshape(-1), q, k_new, v_new, lf_new, *consts,
      *([cache_kt] * pp), *([cache_vt] * pp), *([cache_lft] * pp))


def _np_split3(x):
    x = jnp.asarray(x, F32)
    hi, mid, lo = _split3(x)
    return hi, mid, lo


def _alibi_slopes():
    return 2.0 ** (-8.0 * np.arange(1, N_HEADS + 1) / N_HEADS)


def _prompt_tables(seq, tm):
    n_blk = seq // MOBA_BLOCK
    pos = np.zeros((n_blk, MOBA_BLOCK, PAIR), np.float32)
    pos[:, :, 0:3] = np.arange(MOBA_BLOCK, dtype=np.float32)[None, :, None]
    pos[:, :, 3:6] = (np.arange(n_blk, dtype=np.float32) * MOBA_BLOCK)[:, None, None]
    s_hi, s_mid, s_lo = _np_split3(_alibi_slopes())
    parts = jnp.stack([s_hi, s_mid, s_lo, s_hi, s_mid, s_lo], axis=1).astype(F32)
    qa_moba = jnp.zeros((N_HEADS, PAIR, MOBA_BLOCK), F32)
    qa_moba = qa_moba.at[:, 0:6, :].set(jnp.broadcast_to(parts[:, :, None], (N_HEADS, 6, MOBA_BLOCK)))
    qa_fox = np.zeros((N_HEADS, PAIR, MOBA_BLOCK), np.float32)
    for h in range(N_HEADS):
        i = h % 2
        qa_fox[h, 3 * i:3 * i + 3, :] = -1.0
    place = np.zeros((LANES, D_MODEL), np.float32)
    for h in range(N_HEADS):
        for piece in range(3):
            place[piece * N_HEADS + h, (h // 2) * PAIR + 3 * (h % 2) + piece] = 1.0
    tri = np.tril(np.ones((tm, tm), np.float32))
    hm = (np.arange(D_MODEL)[None, :] // HEAD_DIM == np.arange(N_HEADS)[:, None]).astype(np.float32)
    return dict(pos=jnp.asarray(pos, BF16), qa_moba=qa_moba.astype(BF16), qa_fox=jnp.asarray(qa_fox, BF16),
                place=jnp.asarray(place, BF16), tri=jnp.asarray(tri, BF16), hm=jnp.asarray(hm, F32))


def _sample_tables(t_new, past_len):
    rows = N_HEADS * t_new
    h_of = np.arange(rows) // t_new
    t_of = np.arange(rows) % t_new
    dm = (np.arange(D_MODEL)[None, :] // HEAD_DIM == h_of[:, None]).astype(np.float32)
    slopes = _alibi_slopes().astype(np.float32)
    slb = np.broadcast_to(slopes[h_of][:, None], (rows, MOBA_BLOCK)).astype(np.float32)
    qp = (past_len + t_of[:, None] - np.arange(MOBA_BLOCK)[None, :]).astype(np.float32)
    s_idx = np.arange(LANES)[None, :]
    visible = (s_idx <= t_of[:, None]) & (s_idx < t_new)
    nb_moba = np.where(visible, -slopes[h_of][:, None] * (t_of[:, None] - s_idx), NEG).astype(np.float32)
    nb_fox = np.where(visible, 0.0, NEG).astype(np.float32)
    qaux = np.zeros((rows, LANES), np.float32)
    for piece in range(3):
        qaux[np.arange(rows), piece * N_HEADS + h_of] = 1.0
    tri = np.tril(np.ones((LANES, LANES), np.float32))
    later = np.tril(np.ones((PAGE_SIZE, PAGE_SIZE), np.float32), k=-1)
    ones = np.ones((PAGE_SIZE, PAGE_SIZE), np.float32)
    bf = lambda a: jnp.asarray(a, BF16)
    f32 = lambda a: jnp.asarray(a, F32)
    return dict(
        moba=(f32(dm), f32(slb), f32(qp), f32(nb_moba)),
        fox=(f32(dm), bf(qaux), bf(tri), f32(nb_fox), bf(later), bf(ones)))


def kernel(x_prompt, x_sample, cache_k_moba, cache_v_moba, cache_k_fox, cache_v_fox, cache_logf_fox,
           page_table, w_in_moba, w_out_moba, w_in_fox, b_f_fox, w_out_fox, w_ffn_up, w_ffn_down,
           ln_g, ln_b):
    batch, seq, d = x_prompt.shape
    dec_batch, t_new, _ = x_sample.shape
    n_pool = cache_k_moba.shape[1]
    past_len = page_table.shape[1] * PAGE_SIZE
    assert d == D_MODEL and seq % 512 == 0 and t_new == 8
    tm = 512
    pt = _prompt_tables(seq, tm)
    st = _sample_tables(t_new, past_len)

    yp = x_prompt.reshape(batch * seq, d)
    ys = x_sample.reshape(dec_batch * t_new, d)
    n_s = ys.shape[0]

    def norm(i, j):
        return ln_g[i, j].reshape(1, d), ln_b[i, j].reshape(1, d)

    w_up_bf = w_ffn_up.astype(BF16)
    w_down_bf = w_ffn_down.astype(BF16)

    def ffn_both(yp, ys, i, j):
        g, b = norm(i, 2 * j)
        return (_ffn(yp, w_up_bf, w_down_bf, i, j, g, b, tm), _ffn(ys, w_up_bf, w_down_bf, i, j, g, b, n_s))

    def heads(z, lead):
        return z.reshape(lead + (N_HEADS, HEAD_DIM))[None]

    def heads_t(zt):
        return jnp.transpose(zt.reshape(batch, N_HEADS, HEAD_DIM, seq), (0, 3, 1, 2))[None]

    def pages_t(cache):
        return jnp.transpose(cache, (0, 2, 3, 1)).reshape(n_pool, d, PAGE_SIZE)

    outs = {}
    for i in range(DEPTH):
        li = i // 2
        yp, ys = ffn_both(yp, ys, i, 0)
        if i % 2 == 0:
            w = w_in_moba[li].astype(BF16)
            wq, wk, wv = w[:, :d], w[:, d:2 * d], w[:, 2 * d:3 * d]
            kp, vp, kb, qt, vt, sel = _proj_moba(yp, wk, wk.T, wq.T, wv.T, pt["hm"], batch, seq, tm)
            qs, ks, vs = _proj_nat(ys, wq, wk, wv)
            mt = _attn(kb, pt["pos"], qt, vt, pt["qa_moba"], sel, batch, seq, True)
            ms = _moba_sample(qs, ks, vs, pages_t(cache_k_moba[li]), pages_t(cache_v_moba[li]),
                              page_table, st["moba"], t_new)
            w_out = w_out_moba[li].astype(BF16)
            outs["kmp"], outs["vmp"] = heads_t(kp), heads_t(vp)
            outs["kms"], outs["vms"] = heads(ks, (dec_batch, t_new)), heads(vs, (dec_batch, t_new))
        else:
            w = w_in_fox[li].astype(BF16)
            wq, wk, wv = w[:, :d], w[:, d:2 * d], w[:, 2 * d:3 * d]
            wf = jnp.pad(w[:, 3 * d:], ((0, 0), (0, LANES - N_HEADS)))
            bf = jnp.pad(b_f_fox[li].astype(F32).reshape(1, N_HEADS), ((0, 0), (0, LANES - N_HEADS)))
            kp, vp, kb, qt, vt, lfp, ck = _proj_fox(yp, wk, wk.T, wq.T, wv.T, wf, bf, pt["tri"], pt["place"],
                                                   batch, seq, tm)
            qs, ks, vs, lfs = _proj_nat(ys, wq, wk, wv, wf, bf)
            mt = _attn(kb, ck, qt, vt, pt["qa_fox"], None, batch, seq, False)
            ms = _fox_sample(qs, ks, vs, lfs, pages_t(cache_k_fox[li]), pages_t(cache_v_fox[li]),
                             jnp.transpose(cache_logf_fox[li], (0, 2, 1)), page_table, st["fox"], t_new)
            w_out = w_out_fox[li].astype(BF16)
            outs["kfp"], outs["vfp"] = heads_t(kp), heads_t(vp)
            outs["lfp"] = lfp.reshape(batch, seq, N_HEADS)[None]
            outs["kfs"], outs["vfs"] = heads(ks, (dec_batch, t_new)), heads(vs, (dec_batch, t_new))
            outs["lfs"] = lfs.reshape(dec_batch, t_new, N_HEADS)[None]
        g, b = norm(i, 1)
        yp = _outproj_t(yp, mt, w_out, g, b, tm)
        ys = _outproj(ys, ms, w_out, g, b)
        yp, ys = ffn_both(yp, ys, i, 1)

    return (yp.reshape(batch, seq, d), ys.reshape(dec_batch, t_new, d),
            outs["kmp"], outs["vmp"], outs["kfp"], outs["vfp"], outs["lfp"],
            outs["kms"], outs["vms"], outs["kfs"], outs["vfs"], outs["lfs"])
```

```python
import functools

import numpy as np
import jax
import jax.numpy as jnp
from jax import lax
from jax.experimental import pallas as pl
from jax.experimental.pallas import tpu as pltpu

F32 = jnp.float32
BF16 = jnp.bfloat16

D_MODEL = 1024
N_HEADS = 16
HEAD_DIM = D_MODEL // N_HEADS
D_FF = 2816
DEPTH = 2
MOBA_BLOCK = 256
MOBA_TOPK = 3
PAGE_SIZE = 128
DN_ALPHA = (2.0 * DEPTH) ** 0.25
LN_EPS = 1e-5
NEG = -1e30
QK_SCALE = HEAD_DIM ** -0.5

LANES = 128
PAIR = 2 * HEAD_DIM
N_PAIRS = N_HEADS // 2
FF_CHUNK = 256
PAGES_PER_STEP = 16
ATTN_BLOCKS_PER_TILE = 2
VMEM_LIMIT = 56 * 1024 * 1024

_NT = (((1,), (1,)), ((), ()))


def _dot(a, b):
    return jnp.dot(a, b, preferred_element_type=F32)


def _dot_nt(a, b):
    return lax.dot_general(a, b, _NT, preferred_element_type=F32)


def _split3(x):
    hi = x.astype(BF16)
    r1 = x - hi.astype(F32)
    mid = r1.astype(BF16)
    lo = (r1 - mid.astype(F32)).astype(BF16)
    return hi, mid, lo


def _dot_exact_lhs(a_bf, x):
    hi, mid, lo = _split3(x)
    return _dot(a_bf, hi) + _dot(a_bf, mid) + _dot(a_bf, lo)


def _layer_norm(z, g, b):
    mu = jnp.mean(z, axis=-1, keepdims=True)
    zc = z - mu
    var = jnp.mean(zc * zc, axis=-1, keepdims=True)
    return zc * lax.rsqrt(var + LN_EPS) * g + b


def _log_sigmoid(x):
    return jnp.minimum(x, 0.0) - jnp.log1p(jnp.exp(-jnp.abs(x)))


def _params(*sem):
    return pltpu.CompilerParams(dimension_semantics=sem, vmem_limit_bytes=VMEM_LIMIT)


def _ffn_body(x_ref, wu_ref, wd_ref, g_ref, b_ref, o_ref):
    x = x_ref[...]
    xb = x.astype(BF16)
    acc = None
    for c in range(D_FF // FF_CHUNK):
        cols = slice(c * FF_CHUNK, (c + 1) * FF_CHUNK)
        up_cols = slice(D_FF + c * FF_CHUNK, D_FF + (c + 1) * FF_CHUNK)
        gate = _dot(xb, wu_ref[:, cols])
        up = _dot(xb, wu_ref[:, up_cols])
        h = (gate * jax.nn.sigmoid(gate)) * up
        part = _dot(h.astype(BF16), wd_ref[cols, :])
        acc = part if acc is None else acc + part
    z = DN_ALPHA * x + 0.5 * acc
    o_ref[...] = _layer_norm(z, g_ref[...], b_ref[...])


def _ffn(x, w_up, w_down, layer, half, g, b, tm):
    n = x.shape[0]
    return pl.pallas_call(
        _ffn_body,
        grid=(n // tm,),
        in_specs=[
            pl.BlockSpec((tm, D_MODEL), lambda i: (i, 0)),
            pl.BlockSpec((None, None, D_MODEL, 2 * D_FF), lambda i: (layer, half, 0, 0)),
            pl.BlockSpec((None, None, D_FF, D_MODEL), lambda i: (layer, half, 0, 0)),
            pl.BlockSpec((1, D_MODEL), lambda i: (0, 0)),
            pl.BlockSpec((1, D_MODEL), lambda i: (0, 0)),
        ],
        out_specs=pl.BlockSpec((tm, D_MODEL), lambda i: (i, 0)),
        out_shape=jax.ShapeDtypeStruct((n, D_MODEL), F32),
        compiler_params=_params("parallel"),
        name="ffn",
    )(x, w_up, w_down, g, b)


def _outproj_body(y_ref, m_ref, w_ref, g_ref, b_ref, o_ref):
    sub = _dot(m_ref[...].astype(BF16), w_ref[...])
    z = DN_ALPHA * y_ref[...] + sub
    o_ref[...] = _layer_norm(z, g_ref[...], b_ref[...])


def _outproj(y, m, w_out, g, b, tm):
    n = y.shape[0]
    row = lambda i: (i, 0)
    full = lambda shape: pl.BlockSpec(shape, lambda i: (0, 0))
    return pl.pallas_call(
        _outproj_body,
        grid=(n // tm,),
        in_specs=[pl.BlockSpec((tm, D_MODEL), row), pl.BlockSpec((tm, D_MODEL), row),
                  full((D_MODEL, D_MODEL)), full((1, D_MODEL)), full((1, D_MODEL))],
        out_specs=pl.BlockSpec((tm, D_MODEL), row),
        out_shape=jax.ShapeDtypeStruct((n, D_MODEL), F32),
        compiler_params=_params("parallel"),
        name="outproj",
    )(y, m, w_out, g, b)


def _proj_common(y_ref, wk_ref, wkt_ref, wqt_ref, wvt_ref, k_ref, v_ref, kb_ref, qt_ref, vt_ref, nb):
    yb = y_ref[...].astype(BF16)
    k = _dot(yb, wk_ref[...])
    k_ref[0] = _dot_nt(wkt_ref[...], yb)
    kb = k.astype(BF16)
    qt = _dot_nt(wqt_ref[...], yb)
    vt = _dot_nt(wvt_ref[...], yb)
    v_ref[0] = vt
    for i in range(nb):
        cols = slice(i * MOBA_BLOCK, (i + 1) * MOBA_BLOCK)
        kb_ref[i] = kb[cols, :]
        qt_ref[i] = (qt[:, cols] * QK_SCALE).astype(BF16)
        vt_ref[i] = vt[:, cols].astype(BF16)
    return k, qt


def _proj_moba_body(y_ref, wk_ref, wkt_ref, wqt_ref, wvt_ref, hm_ref,
                    k_ref, v_ref, kb_ref, qt_ref, vt_ref, sel_ref, km_ref, *, nb, n_blk):
    t = pl.program_id(1)

    @pl.when(t == 0)
    def _():
        km_ref[...] = jnp.zeros_like(km_ref)

    k, qt = _proj_common(y_ref, wk_ref, wkt_ref, wqt_ref, wvt_ref,
                         k_ref, v_ref, kb_ref, qt_ref, vt_ref, nb)
    for i in range(nb):
        rows = slice(i * MOBA_BLOCK, (i + 1) * MOBA_BLOCK)
        km_ref[pl.ds(t * nb + i, 1), :] = jnp.sum(k[rows, :], axis=0, keepdims=True) / MOBA_BLOCK

    km = km_ref[...]
    hm = hm_ref[...]
    kmb = jnp.concatenate(
        [jnp.broadcast_to(km[n:n + 1, :], (N_HEADS, D_MODEL)) * hm for n in range(n_blk)], axis=0)
    a_hi = kmb.astype(BF16)
    a_mid = (kmb - a_hi.astype(F32)).astype(BF16)
    q_hi = qt.astype(BF16)
    q_mid = (qt - q_hi.astype(F32)).astype(BF16)
    gate = _dot(a_hi, q_hi) + _dot(a_hi, q_mid) + _dot(a_mid, q_hi)

    tm = nb * MOBA_BLOCK
    col = lax.broadcasted_iota(jnp.int32, (N_HEADS, tm), 1)
    q_blk = t * nb + col // MOBA_BLOCK
    g = []
    for n in range(n_blk):
        g.append(jnp.where(n < q_blk, gate[n * N_HEADS:(n + 1) * N_HEADS, :], NEG))
    bias = []
    for n in range(n_blk):
        rank = jnp.zeros((N_HEADS, tm), jnp.int32)
        for m in range(n_blk):
            if m == n:
                continue
            ahead = (g[m] >= g[n]) if m < n else (g[m] > g[n])
            rank = rank + ahead.astype(jnp.int32)
        keep = (n < q_blk) & (rank < MOBA_TOPK)
        bias.append(jnp.where(keep, 0.0, NEG))
    bias = jnp.concatenate(bias, axis=0)
    for i in range(nb):
        sel_ref[i] = bias[:, i * MOBA_BLOCK:(i + 1) * MOBA_BLOCK]


def _proj_moba(y, wk, wkt, wqt, wvt, hm, batch, seq, tm):
    n = y.shape[0]
    nb = tm // MOBA_BLOCK
    n_blk = seq // MOBA_BLOCK
    tiles = seq // tm
    full = lambda shape: pl.BlockSpec(shape, lambda b, t: (0,) * len(shape))
    row = lambda b, t: (b * tiles + t, 0)
    blk = lambda b, t: (b * tiles + t, 0, 0)
    tcol = lambda b, t: (b, 0, t)
    nblocks = n // MOBA_BLOCK
    return pl.pallas_call(
        functools.partial(_proj_moba_body, nb=nb, n_blk=n_blk),
        grid=(batch, tiles),
        in_specs=[pl.BlockSpec((tm, D_MODEL), row), full((D_MODEL, D_MODEL)), full((D_MODEL, D_MODEL)),
                  full((D_MODEL, D_MODEL)), full((D_MODEL, D_MODEL)), full((N_HEADS, D_MODEL))],
        out_specs=[
            pl.BlockSpec((1, D_MODEL, tm), tcol),
            pl.BlockSpec((1, D_MODEL, tm), tcol),
            pl.BlockSpec((nb, MOBA_BLOCK, D_MODEL), blk),
            pl.BlockSpec((nb, D_MODEL, MOBA_BLOCK), blk),
            pl.BlockSpec((nb, D_MODEL, MOBA_BLOCK), blk),
            pl.BlockSpec((nb, n_blk * N_HEADS, MOBA_BLOCK), blk),
        ],
        out_shape=[
            jax.ShapeDtypeStruct((batch, D_MODEL, seq), F32),
            jax.ShapeDtypeStruct((batch, D_MODEL, seq), F32),
            jax.ShapeDtypeStruct((nblocks, MOBA_BLOCK, D_MODEL), BF16),
            jax.ShapeDtypeStruct((nblocks, D_MODEL, MOBA_BLOCK), BF16),
            jax.ShapeDtypeStruct((nblocks, D_MODEL, MOBA_BLOCK), BF16),
            jax.ShapeDtypeStruct((nblocks, n_blk * N_HEADS, MOBA_BLOCK), F32),
        ],
        scratch_shapes=[pltpu.VMEM((n_blk, D_MODEL), F32)],
        compiler_params=_params("arbitrary", "arbitrary"),
        name="proj_moba",
    )(y, wk, wkt, wqt, wvt, hm)


def _logf_from(yb, wf_ref, bf_ref, tm):
    lane = lax.broadcasted_iota(jnp.int32, (tm, LANES), 1)
    zf = _dot(yb, wf_ref[...]) + bf_ref[...]
    return jnp.where(lane < N_HEADS, _log_sigmoid(zf), 0.0)


def _pack3(x):
    hi, mid, lo = _split3(x)
    packed = (hi.astype(F32) + pltpu.roll(mid.astype(F32), N_HEADS, 1)
              + pltpu.roll(lo.astype(F32), 2 * N_HEADS, 1))
    return packed.astype(BF16)


def _proj_fox_body(y_ref, wk_ref, wkt_ref, wqt_ref, wvt_ref, wf_ref, bf_ref, tri_ref, place_ref,
                   k_ref, v_ref, kb_ref, qt_ref, vt_ref, lf_ref, ck_ref, carry_ref, *, nb):
    t = pl.program_id(1)

    @pl.when(t == 0)
    def _():
        carry_ref[...] = jnp.zeros_like(carry_ref)

    _proj_common(y_ref, wk_ref, wkt_ref, wqt_ref, wvt_ref, k_ref, v_ref, kb_ref, qt_ref, vt_ref, nb)
    tm = nb * MOBA_BLOCK
    yb = y_ref[...].astype(BF16)
    lf = _logf_from(yb, wf_ref, bf_ref, tm)
    lf_ref[...] = lf[:, :N_HEADS]
    ct = _dot_exact_lhs(tri_ref[...], lf) + carry_ref[0:1, :]
    carry_ref[0:1, :] = ct[tm - 1:tm, :]
    ck = _dot(_pack3(ct), place_ref[...]).astype(BF16)
    for i in range(nb):
        ck_ref[i] = ck[i * MOBA_BLOCK:(i + 1) * MOBA_BLOCK, :]


def _proj_fox(y, wk, wkt, wqt, wvt, wf, bf, tri, place, batch, seq, tm):
    n = y.shape[0]
    nb = tm // MOBA_BLOCK
    tiles = seq // tm
    full = lambda shape: pl.BlockSpec(shape, lambda b, t: (0,) * len(shape))
    row = lambda b, t: (b * tiles + t, 0)
    blk = lambda b, t: (b * tiles + t, 0, 0)
    tcol = lambda b, t: (b, 0, t)
    nblocks = n // MOBA_BLOCK
    return pl.pallas_call(
        functools.partial(_proj_fox_body, nb=nb),
        grid=(batch, tiles),
        in_specs=[pl.BlockSpec((tm, D_MODEL), row), full((D_MODEL, D_MODEL)), full((D_MODEL, D_MODEL)),
                  full((D_MODEL, D_MODEL)), full((D_MODEL, D_MODEL)), full((D_MODEL, LANES)),
                  full((1, LANES)), full((tm, tm)), full((LANES, D_MODEL))],
        out_specs=[
            pl.BlockSpec((1, D_MODEL, tm), tcol),
            pl.BlockSpec((1, D_MODEL, tm), tcol),
            pl.BlockSpec((nb, MOBA_BLOCK, D_MODEL), blk),
            pl.BlockSpec((nb, D_MODEL, MOBA_BLOCK), blk),
            pl.BlockSpec((nb, D_MODEL, MOBA_BLOCK), blk),
            pl.BlockSpec((tm, N_HEADS), row),
            pl.BlockSpec((nb, MOBA_BLOCK, D_MODEL), blk),
        ],
        out_shape=[
            jax.ShapeDtypeStruct((batch, D_MODEL, seq), F32),
            jax.ShapeDtypeStruct((batch, D_MODEL, seq), F32),
            jax.ShapeDtypeStruct((nblocks, MOBA_BLOCK, D_MODEL), BF16),
            jax.ShapeDtypeStruct((nblocks, D_MODEL, MOBA_BLOCK), BF16),
            jax.ShapeDtypeStruct((nblocks, D_MODEL, MOBA_BLOCK), BF16),
            jax.ShapeDtypeStruct((n, N_HEADS), F32),
            jax.ShapeDtypeStruct((nblocks, MOBA_BLOCK, D_MODEL), BF16),
        ],
        scratch_shapes=[pltpu.VMEM((8, LANES), F32)],
        compiler_params=_params("arbitrary", "arbitrary"),
        name="proj_fox",
    )(y, wk, wkt, wqt, wvt, wf, bf, tri, place)


def _proj_nat_body(y_ref, wq_ref, wk_ref, wv_ref, *rest, with_logf):
    if with_logf:
        wf_ref, bf_ref, q_ref, k_ref, v_ref, lf_ref = rest
    else:
        q_ref, k_ref, v_ref = rest
    yb = y_ref[...].astype(BF16)
    q_ref[...] = _dot(yb, wq_ref[...])
    k_ref[...] = _dot(yb, wk_ref[...])
    v_ref[...] = _dot(yb, wv_ref[...])
    if with_logf:
        lf_ref[...] = _logf_from(yb, wf_ref, bf_ref, y_ref.shape[0])[:, :N_HEADS]


def _proj_nat(y, wq, wk, wv, wf=None, bf=None):
    n = y.shape[0]
    with_logf = wf is not None
    full = lambda shape: pl.BlockSpec(shape, lambda i: (0,) * len(shape))
    ins = [y, wq, wk, wv]
    in_specs = [full((n, D_MODEL))] + [full((D_MODEL, D_MODEL))] * 3
    out_specs = [full((n, D_MODEL))] * 3
    out_shape = [jax.ShapeDtypeStruct((n, D_MODEL), F32)] * 3
    if with_logf:
        ins += [wf, bf]
        in_specs += [full((D_MODEL, LANES)), full((1, LANES))]
        out_specs += [full((n, N_HEADS))]
        out_shape += [jax.ShapeDtypeStruct((n, N_HEADS), F32)]
    return pl.pallas_call(
        functools.partial(_proj_nat_body, with_logf=with_logf),
        grid=(1,),
        in_specs=in_specs,
        out_specs=out_specs,
        out_shape=out_shape,
        compiler_params=_params("arbitrary"),
        name="proj_nat",
    )(*ins)


def _attn_body(kb_ref, aux_ref, qt_ref, vt_ref, qa_ref, *rest, n_blk, use_sel, sb):
    if use_sel:
        sel_ref, o_ref, s_ref = rest
    else:
        o_ref, s_ref = rest
    hp = pl.program_id(1)
    tile = sb * MOBA_BLOCK
    key_i = lax.broadcasted_iota(jnp.int32, (tile, tile), 0)
    qry_i = lax.broadcasted_iota(jnp.int32, (tile, tile), 1)
    causal = key_i <= qry_i
    rid = lax.broadcasted_iota(jnp.int32, (PAIR, tile), 0)
    hrows = [slice(i * HEAD_DIM, (i + 1) * HEAD_DIM) for i in range(2)]

    def cat(parts, axis):
        return parts[0] if len(parts) == 1 else jnp.concatenate(parts, axis=axis)

    def keys_of(t):
        kb = cat([kb_ref[t * sb + a] for a in range(sb)], 0)
        aux = cat([aux_ref[t * sb + a] for a in range(sb)], 0)
        return jnp.concatenate([kb, aux], axis=1)

    def vt_of(t, i):
        return cat([vt_ref[t * sb + a, hrows[i], :] for a in range(sb)], 1)

    def sel_rows(jt, kt, i, a, first_c):
        row = (kt * sb + a) * N_HEADS + 2 * hp + i
        parts = [sel_ref[jt * sb + c, pl.ds(row, 1), :] if c >= first_c
                 else jnp.zeros((1, MOBA_BLOCK), F32) for c in range(sb)]
        return cat(parts, 1)

    def add_sel(s, jt, kt, i, diag):
        if not use_sel:
            return s
        rows = []
        for a in range(sb):
            sa = s[a * MOBA_BLOCK:(a + 1) * MOBA_BLOCK, :]
            if diag and a == sb - 1:
                rows.append(sa)
            else:
                rows.append(sa + sel_rows(jt, kt, i, a, a + 1 if diag else 0))
        return cat(rows, 0)

    def q_tile(jt, carry):
        qt = cat([qt_ref[jt * sb + c] for c in range(sb)], 1)
        qaug = []
        for i in range(2):
            own = (rid >= i * HEAD_DIM) & (rid < (i + 1) * HEAD_DIM)
            qa = cat([qa_ref[i]] * sb, 1)
            qaug.append(jnp.concatenate([jnp.where(own, qt, jnp.zeros_like(qt)), qa], axis=0))

        def produce(kt, slot):
            kcat = keys_of(kt)
            for i in range(2):
                s_ref[slot, i] = _dot(kcat, qaug[i])

        def consume(c, kt, slot, diag):
            new = []
            for i in range(2):
                m, l, acc = c[3 * i:3 * i + 3]
                s = add_sel(s_ref[slot, i], jt, kt, i, diag)
                if diag:
                    s = jnp.where(causal, s, NEG)
                m_new = jnp.maximum(m, jnp.max(s, axis=0, keepdims=True))
                corr = jnp.exp(m - m_new)
                p = jnp.exp(s - m_new)
                l = l * corr + jnp.sum(p, axis=0, keepdims=True)
                acc = acc * corr + _dot(vt_of(kt, i), p.astype(BF16))
                new += [m_new, l, acc]
            return tuple(new)

        produce(0, 0)
        state = []
        for i in range(2):
            state += [jnp.full((1, tile), NEG, F32), jnp.zeros((1, tile), F32), jnp.zeros((HEAD_DIM, tile), F32)]

        def past(kt, c):
            slot = kt % 2
            c = consume(c, kt, slot, False)
            produce(kt + 1, 1 - slot)
            return c

        state = lax.fori_loop(0, jt, past, tuple(state))
        state = consume(state, jt, jt % 2, True)
        out = jnp.concatenate([state[3 * i + 2] / state[3 * i + 1] for i in range(2)], axis=0)
        o_ref[pl.ds(pl.multiple_of(jt * tile, tile), tile), :] = out.T.astype(BF16)
        return carry

    lax.fori_loop(0, n_blk // sb, q_tile, 0)


def _attn(kb, aux, qt, vt, qa, sel, batch, seq, aux_is_const):
    n_blk = seq // MOBA_BLOCK
    nblocks = kb.shape[0]
    use_sel = sel is not None
    aux_map = (lambda b, hp: (0, 0, 0)) if aux_is_const else (lambda b, hp: (b, 0, hp))
    in_specs = [
        pl.BlockSpec((n_blk, MOBA_BLOCK, PAIR), lambda b, hp: (b, 0, hp)),
        pl.BlockSpec((n_blk, MOBA_BLOCK, PAIR), aux_map),
        pl.BlockSpec((n_blk, PAIR, MOBA_BLOCK), lambda b, hp: (b, hp, 0)),
        pl.BlockSpec((n_blk, PAIR, MOBA_BLOCK), lambda b, hp: (b, hp, 0)),
        pl.BlockSpec((2, PAIR, MOBA_BLOCK), lambda b, hp: (hp, 0, 0)),
    ]
    ins = [kb, aux, qt, vt, qa]
    if use_sel:
        in_specs.append(pl.BlockSpec((n_blk, n_blk * N_HEADS, MOBA_BLOCK), lambda b, hp: (b, 0, 0)))
        ins.append(sel)
    return pl.pallas_call(
        functools.partial(_attn_body, n_blk=n_blk, use_sel=use_sel, sb=ATTN_BLOCKS_PER_TILE),
        grid=(batch, N_PAIRS),
        in_specs=in_specs,
        out_specs=pl.BlockSpec((seq, PAIR), lambda b, hp: (b, hp)),
        out_shape=jax.ShapeDtypeStruct((batch * seq, D_MODEL), BF16),
        scratch_shapes=[pltpu.VMEM((2, 2, ATTN_BLOCKS_PER_TILE * MOBA_BLOCK,
                                    ATTN_BLOCKS_PER_TILE * MOBA_BLOCK), F32)],
        compiler_params=_params("parallel", "parallel"),
        name="attn_moba" if use_sel else "attn_fox",
    )(*ins)


def _diag_rows(x, t):
    return jnp.sum(x.reshape(N_HEADS, t, D_MODEL), axis=0)


def _dot_exact_rhs(x, b_bf):
    hi, mid, lo = _split3(x)
    return _dot(hi, b_bf) + _dot(mid, b_bf) + _dot(lo, b_bf)


def _q_rows(q_ref, dm, scale):
    return jnp.concatenate([q_ref[...]] * N_HEADS, axis=0) * (dm * scale)


def _new_token_scores(qbt, kn_ref, vn_ref, nbias, extra_cols, t_new):
    pad = jnp.zeros((LANES - t_new, D_MODEL), F32)
    kn = jnp.concatenate([kn_ref[...], pad], axis=0).astype(BF16)
    vn = jnp.concatenate([vn_ref[...], pad], axis=0).astype(BF16)
    if extra_cols is not None:
        kn = jnp.concatenate([kn, extra_cols], axis=1)
    return _dot_nt(qbt, kn) + nbias, vn


def _moba_sample_body(pt_ref, q_ref, kn_ref, vn_ref, dm_ref, slb_ref, qp_ref, nbias_ref,
                      *rest, pp, t_new, n_blk):
    del pt_ref
    k_refs, v_refs = rest[:pp], rest[pp:2 * pp]
    o_ref, qbt_ref, qmid_ref, g_ref, m_ref, l_ref, od_ref = rest[2 * pp:]
    s = pl.program_id(1)
    rows = N_HEADS * t_new
    dm = dm_ref[...]
    lane = lax.broadcasted_iota(jnp.int32, (rows, LANES), 1)

    @pl.when(s == 0)
    def _():
        q16 = _q_rows(q_ref, dm, QK_SCALE)
        hi = q16.astype(BF16)
        qbt_ref[...] = hi
        qmid_ref[...] = (q16 - hi.astype(F32)).astype(BF16)
        g_ref[...] = jnp.zeros_like(g_ref)
        m_ref[...] = jnp.zeros_like(m_ref)
        l_ref[...] = jnp.zeros_like(l_ref)

    qbt = qbt_ref[...]
    per_blk = MOBA_BLOCK // PAGE_SIZE
    pair_rows = 2 * t_new
    first_head = lax.broadcasted_iota(jnp.int32, (t_new, LANES), 1) < HEAD_DIM

    def pair_q(ref, g):
        return ref[g * pair_rows:(g + 1) * pair_rows, g * PAIR:(g + 1) * PAIR]

    g_new = m_new = l_new = jnp.zeros((rows, LANES), F32)
    for i in range(pp // per_blk):
        n = s * (pp // per_blk) + i
        kf = jnp.concatenate([k_refs[per_blk * i + j][...] for j in range(per_blk)], axis=1)
        kt = kf.astype(BF16)
        vt = jnp.concatenate([v_refs[per_blk * i + j][...] for j in range(per_blk)], axis=1).astype(BF16)
        ks = kf[:, :LANES] + kf[:, LANES:]
        ks_hi = ks.astype(BF16)
        ks_mid = (ks - ks_hi.astype(F32)).astype(BF16)
        sc, gsum = [], []
        for g in range(N_PAIRS):
            feat = slice(g * PAIR, (g + 1) * PAIR)
            qg, qg_mid = pair_q(qbt_ref, g), pair_q(qmid_ref, g)
            sc.append(_dot(qg, kt[feat, :]))
            gsum.append(jnp.sum(_dot(qg, ks_hi[feat, :]), axis=1, keepdims=True)
                        + jnp.sum(_dot(qg_mid, ks_hi[feat, :]), axis=1, keepdims=True)
                        + jnp.sum(_dot(qg, ks_mid[feat, :]), axis=1, keepdims=True))
        sc = jnp.concatenate(sc, axis=0)
        gsum = jnp.concatenate(gsum, axis=0)
        dist = qp_ref[...] - (n * MOBA_BLOCK).astype(F32)
        sc = sc - slb_ref[...] * dist
        m = jnp.max(sc, axis=1, keepdims=True)
        e = jnp.exp(sc - m)
        l = jnp.sum(e, axis=1, keepdims=True)
        eb = e.astype(BF16)
        od_ref[n] = jnp.concatenate(
            [_dot_nt(eb[g * pair_rows:(g + 1) * pair_rows, :], vt[g * PAIR:(g + 1) * PAIR, :])
             for g in range(N_PAIRS)], axis=0)
        onehot = (lane == n).astype(F32)
        g_new = g_new + gsum * onehot
        m_new = m_new + m * onehot
        l_new = l_new + l * onehot
    g_ref[...] += g_new
    m_ref[...] += m_new
    l_ref[...] += l_new

    @pl.when(s == pl.num_programs(1) - 1)
    def _():
        low = jnp.float32(-3.0e38)
        g = jnp.where(lane < n_blk, g_ref[...], low)
        lane_f = lane.astype(F32)
        sel = lane < 0
        for _ in range(MOBA_TOPK):
            mx = jnp.max(g, axis=1, keepdims=True)
            first = jnp.min(jnp.where(g == mx, lane_f, float(LANES)), axis=1, keepdims=True)
            pick = lane_f == first
            sel = sel | pick
            g = jnp.where(pick, low, g)
        sn, vn = _new_token_scores(qbt, kn_ref, vn_ref, nbias_ref[...], None, t_new)
        m_n = jnp.max(sn, axis=1, keepdims=True)
        e_n = jnp.exp(sn - m_n)
        l_n = jnp.sum(e_n, axis=1, keepdims=True)
        o_n = _dot(e_n.astype(BF16), vn)
        od_n = jnp.concatenate([o_n[g * pair_rows:(g + 1) * pair_rows, g * PAIR:(g + 1) * PAIR]
                                for g in range(N_PAIRS)], axis=0)
        m_all = m_ref[...]
        m_tot = jnp.maximum(jnp.max(jnp.where(sel, m_all, NEG), axis=1, keepdims=True), m_n)
        w = jnp.where(sel, jnp.exp(jnp.minimum(m_all - m_tot, 0.0)), 0.0)
        w_n = jnp.exp(m_n - m_tot)
        l_tot = jnp.sum(w * l_ref[...], axis=1, keepdims=True) + w_n * l_n
        acc = w_n * od_n
        for n in range(n_blk):
            acc = acc + w[:, n:n + 1] * od_ref[n]
        out = acc / l_tot
        o_ref[...] = jnp.concatenate(
            [jnp.where(first_head, out[g * pair_rows:g * pair_rows + t_new, :],
                       out[g * pair_rows + t_new:(g + 1) * pair_rows, :]) for g in range(N_PAIRS)], axis=1)


def _page_spec(i, pp, n_pages, reverse):
    def index(b, s, pt):
        p = s * pp + i
        if reverse:
            p = n_pages - 1 - p
        return (pt[b * n_pages + p], 0, 0)
    return index


def _moba_sample(q, k_new, v_new, cache_kt, cache_vt, page_table, consts, t_new):
    batch = q.shape[0] // t_new
    n_pages = page_table.shape[1]
    n_blk = n_pages * PAGE_SIZE // MOBA_BLOCK
    pp = PAGES_PER_STEP
    rows = N_HEADS * t_new
    tok = pl.BlockSpec((t_new, D_MODEL), lambda b, s, pt: (b, 0))
    const = lambda a: pl.BlockSpec(a.shape, lambda b, s, pt: (0, 0))
    page = lambda i: pl.BlockSpec((None, D_MODEL, PAGE_SIZE), _page_spec(i, pp, n_pages, False))
    grid_spec = pltpu.PrefetchScalarGridSpec(
        num_scalar_prefetch=1,
        grid=(batch, n_pages // pp),
        in_specs=[tok, tok, tok] + [const(a) for a in consts]
        + [page(i) for i in range(pp)] + [page(i) for i in range(pp)],
        out_specs=tok,
        scratch_shapes=[
            pltpu.VMEM((rows, D_MODEL), BF16),
            pltpu.VMEM((rows, D_MODEL), BF16),
            pltpu.VMEM((rows, LANES), F32),
            pltpu.VMEM((rows, LANES), F32),
            pltpu.VMEM((rows, LANES), F32),
            pltpu.VMEM((n_blk, rows, LANES), F32),
        ],
    )
    return pl.pallas_call(
        functools.partial(_moba_sample_body, pp=pp, t_new=t_new, n_blk=n_blk),
        grid_spec=grid_spec,
        out_shape=jax.ShapeDtypeStruct((batch * t_new, D_MODEL), F32),
        compiler_params=_params("parallel", "arbitrary"),
        name="moba_sample",
    )(page_table.reshape(-1), q, k_new, v_new, *consts,
      *([cache_kt] * pp), *([cache_vt] * pp))


def _fox_sample_body(pt_ref, q_ref, kn_ref, vn_ref, lfn_ref, dm_ref, qaux_ref, tri_ref, nbias_ref,
                     later_ref, ones_ref, *rest, pp, t_new):
    del pt_ref
    k_refs, v_refs, lf_refs = rest[:pp], rest[pp:2 * pp], rest[2 * pp:3 * pp]
    o_ref, qbt_ref, pad_ref, carry_ref, m_ref, l_ref, acc_ref = rest[3 * pp:]
    s = pl.program_id(1)
    dm = dm_ref[...]

    @pl.when(s == 0)
    def _():
        qbt_ref[:, :D_MODEL] = _q_rows(q_ref, dm, QK_SCALE).astype(BF16)
        qbt_ref[:, D_MODEL:] = qaux_ref[...]
        carry_ref[...] = jnp.zeros_like(carry_ref)
        m_ref[...] = jnp.full_like(m_ref, NEG)
        l_ref[...] = jnp.zeros_like(l_ref)
        acc_ref[...] = jnp.zeros_like(acc_ref)

    qbt = qbt_ref[...]

    def online(sc, weigh):
        m_old = m_ref[:, 0:1]
        m_new = jnp.maximum(m_old, jnp.max(sc, axis=1, keepdims=True))
        corr = jnp.exp(m_old - m_new)
        e = jnp.exp(sc - m_new)
        l_ref[...] = jnp.broadcast_to(l_ref[:, 0:1] * corr + jnp.sum(e, axis=1, keepdims=True), l_ref.shape)
        acc_ref[...] = acc_ref[...] * corr + weigh(e.astype(BF16))
        m_ref[...] = jnp.broadcast_to(m_new, m_ref.shape)

    zrows = jnp.zeros((LANES - 3 * N_HEADS, PAGE_SIZE), BF16)
    carry = carry_ref[...]
    kcols = []
    for i in range(pp):
        lft = lf_refs[i][...]
        suf = _dot_exact_rhs(lft, later_ref[...]) + carry
        carry = carry + _dot_exact_rhs(lft, ones_ref[...])
        hi, mid, lo = _split3(suf)
        kcols.append(jnp.concatenate([k_refs[i][...].astype(BF16), hi, mid, lo, zrows], axis=0))
    carry_ref[...] = carry
    kcat = jnp.concatenate(kcols, axis=1)
    vt = jnp.concatenate([v_refs[i][...].astype(BF16) for i in range(pp)], axis=1)
    online(_dot(qbt, kcat), lambda e: _dot_nt(e, vt))

    @pl.when(s == pl.num_programs(1) - 1)
    def _():
        pad_ref[...] = jnp.zeros_like(pad_ref)
        pad_ref[0:t_new, :N_HEADS] = lfn_ref[...]
        c_new = _dot_exact_lhs(tri_ref[...], pad_ref[...])
        sn, vn = _new_token_scores(qbt, kn_ref, vn_ref, nbias_ref[...], _pack3(-c_new), t_new)
        online(sn, lambda e: _dot(e, vn))
        out = acc_ref[...] / l_ref[:, 0:1]
        o_ref[...] = _diag_rows(out * dm, t_new)


def _fox_sample(q, k_new, v_new, lf_new, cache_kt, cache_vt, cache_lft, page_table, consts, t_new):
    batch = q.shape[0] // t_new
    n_pages = page_table.shape[1]
    pp = PAGES_PER_STEP
    rows = N_HEADS * t_new
    tok = pl.BlockSpec((t_new, D_MODEL), lambda b, s, pt: (b, 0))
    const = lambda a: pl.BlockSpec(a.shape, lambda b, s, pt: (0, 0))
    page = lambda i: pl.BlockSpec((None, D_MODEL, PAGE_SIZE), _page_spec(i, pp, n_pages, True))
    lfpage = lambda i: pl.BlockSpec((None, N_HEADS, PAGE_SIZE), _page_spec(i, pp, n_pages, True))
    grid_spec = pltpu.PrefetchScalarGridSpec(
        num_scalar_prefetch=1,
        grid=(batch, n_pages // pp),
        in_specs=[tok, tok, tok, pl.BlockSpec((t_new, N_HEADS), lambda b, s, pt: (b, 0))]
        + [const(a) for a in consts]
        + [page(i) for i in range(pp)] + [page(i) for i in range(pp)] + [lfpage(i) for i in range(pp)],
        out_specs=tok,
        scratch_shapes=[
            pltpu.VMEM((rows, D_MODEL + LANES), BF16),
            pltpu.VMEM((LANES, LANES), F32),
            pltpu.VMEM((N_HEADS, PAGE_SIZE), F32),
            pltpu.VMEM((rows, LANES), F32),
            pltpu.VMEM((rows, LANES), F32),
            pltpu.VMEM((rows, D_MODEL), F32),
        ],
    )
    return pl.pallas_call(
        functools.partial(_fox_sample_body, pp=pp, t_new=t_new),
        grid_spec=grid_spec,
        out_shape=jax.ShapeDtypeStruct((batch * t_new, D_MODEL), F32),
        compiler_params=_params("parallel", "arbitrary"),
        name="fox_sample",
    )(page_table.reshape(-1), q, k_new, v_new, lf_new, *consts,
      *([cache_kt] * pp), *([cache_vt] * pp), *([cache_lft] * pp))


def _np_split3(x):
    x = jnp.asarray(x, F32)
    hi, mid, lo = _split3(x)
    return hi, mid, lo


def _alibi_slopes():
    return 2.0 ** (-8.0 * np.arange(1, N_HEADS + 1) / N_HEADS)


def _prompt_tables(seq, tm):
    n_blk = seq // MOBA_BLOCK
    pos = np.zeros((n_blk, MOBA_BLOCK, PAIR), np.float32)
    pos[:, :, 0:3] = np.arange(MOBA_BLOCK, dtype=np.float32)[None, :, None]
    pos[:, :, 3:6] = (np.arange(n_blk, dtype=np.float32) * MOBA_BLOCK)[:, None, None]
    s_hi, s_mid, s_lo = _np_split3(_alibi_slopes())
    parts = jnp.stack([s_hi, s_mid, s_lo, s_hi, s_mid, s_lo], axis=1).astype(F32)
    qa_moba = jnp.zeros((N_HEADS, PAIR, MOBA_BLOCK), F32)
    qa_moba = qa_moba.at[:, 0:6, :].set(jnp.broadcast_to(parts[:, :, None], (N_HEADS, 6, MOBA_BLOCK)))
    qa_fox = np.zeros((N_HEADS, PAIR, MOBA_BLOCK), np.float32)
    for h in range(N_HEADS):
        i = h % 2
        qa_fox[h, 3 * i:3 * i + 3, :] = -1.0
    place = np.zeros((LANES, D_MODEL), np.float32)
    for h in range(N_HEADS):
        for piece in range(3):
            place[piece * N_HEADS + h, (h // 2) * PAIR + 3 * (h % 2) + piece] = 1.0
    tri = np.tril(np.ones((tm, tm), np.float32))
    hm = (np.arange(D_MODEL)[None, :] // HEAD_DIM == np.arange(N_HEADS)[:, None]).astype(np.float32)
    return dict(pos=jnp.asarray(pos, BF16), qa_moba=qa_moba.astype(BF16), qa_fox=jnp.asarray(qa_fox, BF16),
                place=jnp.asarray(place, BF16), tri=jnp.asarray(tri, BF16), hm=jnp.asarray(hm, F32))


def _sample_tables(t_new, past_len):
    rows = N_HEADS * t_new
    h_of = np.arange(rows) // t_new
    t_of = np.arange(rows) % t_new
    dm = (np.arange(D_MODEL)[None, :] // HEAD_DIM == h_of[:, None]).astype(np.float32)
    slopes = _alibi_slopes().astype(np.float32)
    slb = np.broadcast_to(slopes[h_of][:, None], (rows, MOBA_BLOCK)).astype(np.float32)
    qp = (past_len + t_of[:, None] - np.arange(MOBA_BLOCK)[None, :]).astype(np.float32)
    s_idx = np.arange(LANES)[None, :]
    visible = (s_idx <= t_of[:, None]) & (s_idx < t_new)
    nb_moba = np.where(visible, -slopes[h_of][:, None] * (t_of[:, None] - s_idx), NEG).astype(np.float32)
    nb_fox = np.where(visible, 0.0, NEG).astype(np.float32)
    qaux = np.zeros((rows, LANES), np.float32)
    for piece in range(3):
        qaux[np.arange(rows), piece * N_HEADS + h_of] = 1.0
    tri = np.tril(np.ones((LANES, LANES), np.float32))
    later = np.tril(np.ones((PAGE_SIZE, PAGE_SIZE), np.float32), k=-1)
    ones = np.ones((PAGE_SIZE, PAGE_SIZE), np.float32)
    bf = lambda a: jnp.asarray(a, BF16)
    f32 = lambda a: jnp.asarray(a, F32)
    return dict(
        moba=(f32(dm), f32(slb), f32(qp), f32(nb_moba)),
        fox=(f32(dm), bf(qaux), bf(tri), f32(nb_fox), bf(later), bf(ones)))


def kernel(x_prompt, x_sample, cache_k_moba, cache_v_moba, cache_k_fox, cache_v_fox, cache_logf_fox,
           page_table, w_in_moba, w_out_moba, w_in_fox, b_f_fox, w_out_fox, w_ffn_up, w_ffn_down,
           ln_g, ln_b):
    batch, seq, d = x_prompt.shape
    dec_batch, t_new, _ = x_sample.shape
    n_pool = cache_k_moba.shape[1]
    past_len = page_table.shape[1] * PAGE_SIZE
    assert d == D_MODEL and seq % 512 == 0 and t_new == 8
    tm = 512
    pt = _prompt_tables(seq, tm)
    st = _sample_tables(t_new, past_len)

    yp = x_prompt.reshape(batch * seq, d)
    ys = x_sample.reshape(dec_batch * t_new, d)
    n_s = ys.shape[0]

    def norm(i, j):
        return ln_g[i, j].reshape(1, d), ln_b[i, j].reshape(1, d)

    w_up_bf = w_ffn_up.astype(BF16)
    w_down_bf = w_ffn_down.astype(BF16)

    def ffn_both(yp, ys, i, j):
        g, b = norm(i, 2 * j)
        return (_ffn(yp, w_up_bf, w_down_bf, i, j, g, b, tm), _ffn(ys, w_up_bf, w_down_bf, i, j, g, b, n_s))

    def heads(z, lead):
        return z.reshape(lead + (N_HEADS, HEAD_DIM))[None]

    def heads_t(zt):
        return jnp.transpose(zt.reshape(batch, N_HEADS, HEAD_DIM, seq), (0, 3, 1, 2))[None]

    def pages_t(cache):
        return jnp.transpose(cache, (0, 2, 3, 1)).reshape(n_pool, d, PAGE_SIZE)

    outs = {}
    for i in range(DEPTH):
        li = i // 2
        yp, ys = ffn_both(yp, ys, i, 0)
        if i % 2 == 0:
            w = w_in_moba[li].astype(BF16)
            wq, wk, wv = w[:, :d], w[:, d:2 * d], w[:, 2 * d:3 * d]
            kp, vp, kb, qt, vt, sel = _proj_moba(yp, wk, wk.T, wq.T, wv.T, pt["hm"], batch, seq, tm)
            qs, ks, vs = _proj_nat(ys, wq, wk, wv)
            mp = _attn(kb, pt["pos"], qt, vt, pt["qa_moba"], sel, batch, seq, True)
            ms = _moba_sample(qs, ks, vs, pages_t(cache_k_moba[li]), pages_t(cache_v_moba[li]),
                              page_table, st["moba"], t_new)
            w_out = w_out_moba[li].astype(BF16)
            outs["kmp"], outs["vmp"] = heads_t(kp), heads_t(vp)
            outs["kms"], outs["vms"] = heads(ks, (dec_batch, t_new)), heads(vs, (dec_batch, t_new))
        else:
            w = w_in_fox[li].astype(BF16)
            wq, wk, wv = w[:, :d], w[:, d:2 * d], w[:, 2 * d:3 * d]
            wf = jnp.pad(w[:, 3 * d:], ((0, 0), (0, LANES - N_HEADS)))
            bf = jnp.pad(b_f_fox[li].astype(F32).reshape(1, N_HEADS), ((0, 0), (0, LANES - N_HEADS)))
            kp, vp, kb, qt, vt, lfp, ck = _proj_fox(yp, wk, wk.T, wq.T, wv.T, wf, bf, pt["tri"], pt["place"],
                                                   batch, seq, tm)
            qs, ks, vs, lfs = _proj_nat(ys, wq, wk, wv, wf, bf)
            mp = _attn(kb, ck, qt, vt, pt["qa_fox"], None, batch, seq, False)
            ms = _fox_sample(qs, ks, vs, lfs, pages_t(cache_k_fox[li]), pages_t(cache_v_fox[li]),
                             jnp.transpose(cache_logf_fox[li], (0, 2, 1)), page_table, st["fox"], t_new)
            w_out = w_out_fox[li].astype(BF16)
            outs["kfp"], outs["vfp"] = heads_t(kp), heads_t(vp)
            outs["lfp"] = lfp.reshape(batch, seq, N_HEADS)[None]
            outs["kfs"], outs["vfs"] = heads(ks, (dec_batch, t_new)), heads(vs, (dec_batch, t_new))
            outs["lfs"] = lfs.reshape(dec_batch, t_new, N_HEADS)[None]
        g, b = norm(i, 1)
        yp = _outproj(yp, mp, w_out, g, b, tm)
        ys = _outproj(ys, ms, w_out, g, b, n_s)
        yp, ys = ffn_both(yp, ys, i, 1)

    return (yp.reshape(batch, seq, d), ys.reshape(dec_batch, t_new, d),
            outs["kmp"], outs["vmp"], outs["kfp"], outs["vfp"], outs["lfp"],
            outs["kms"], outs["vms"], outs["kfs"], outs["vfs"], outs["lfs"])
```

```python
import functools

import numpy as np
import jax
import jax.numpy as jnp
from jax import lax
from jax.experimental import pallas as pl
from jax.experimental.pallas import tpu as pltpu

F32 = jnp.float32
BF16 = jnp.bfloat16

D_MODEL = 1024
N_HEADS = 16
HEAD_DIM = D_MODEL // N_HEADS
D_FF = 2816
DEPTH = 2
MOBA_BLOCK = 256
MOBA_TOPK = 3
PAGE_SIZE = 128
DN_ALPHA = (2.0 * DEPTH) ** 0.25
LN_EPS = 1e-5
NEG = -1e30
QK_SCALE = HEAD_DIM ** -0.5
LOG2E = 1.4426950408889634

LANES = 128
PAIR = 2 * HEAD_DIM
N_PAIRS = N_HEADS // 2
FF_CHUNK = 256
FFN_TOKENS = 512
PAGES_PER_STEP = 16
ATTN_BLOCKS_PER_TILE = 2
VMEM_LIMIT = 56 * 1024 * 1024

_NT = (((1,), (1,)), ((), ()))


def _dot(a, b):
    return jnp.dot(a, b, preferred_element_type=F32)


def _dot_nt(a, b):
    return lax.dot_general(a, b, _NT, preferred_element_type=F32)


def _split3(x):
    hi = x.astype(BF16)
    r1 = x - hi.astype(F32)
    mid = r1.astype(BF16)
    lo = (r1 - mid.astype(F32)).astype(BF16)
    return hi, mid, lo


def _dot_exact_lhs(a_bf, x):
    hi, mid, lo = _split3(x)
    return _dot(a_bf, hi) + _dot(a_bf, mid) + _dot(a_bf, lo)


def _layer_norm(z, g, b):
    mu = jnp.mean(z, axis=-1, keepdims=True)
    zc = z - mu
    var = jnp.mean(zc * zc, axis=-1, keepdims=True)
    return zc * lax.rsqrt(var + LN_EPS) * g + b


def _log_sigmoid(x):
    return jnp.minimum(x, 0.0) - jnp.log1p(jnp.exp(-jnp.abs(x)))


def _params(*sem):
    return pltpu.CompilerParams(dimension_semantics=sem, vmem_limit_bytes=VMEM_LIMIT)


def _ffn_body(x_ref, wu_ref, wd_ref, g_ref, b_ref, o_ref):
    x = x_ref[...]
    xb = x.astype(BF16)
    acc = None
    for c in range(D_FF // FF_CHUNK):
        cols = slice(c * FF_CHUNK, (c + 1) * FF_CHUNK)
        up_cols = slice(D_FF + c * FF_CHUNK, D_FF + (c + 1) * FF_CHUNK)
        gate = _dot(xb, wu_ref[:, cols])
        up = _dot(xb, wu_ref[:, up_cols])
        h = (gate * jax.nn.sigmoid(gate)) * up
        part = _dot(h.astype(BF16), wd_ref[cols, :])
        acc = part if acc is None else acc + part
    z = DN_ALPHA * x + 0.5 * acc
    o_ref[...] = _layer_norm(z, g_ref[...], b_ref[...])


def _ffn(x, w_up, w_down, layer, half, g, b, tm):
    n = x.shape[0]
    return pl.pallas_call(
        _ffn_body,
        grid=(n // tm,),
        in_specs=[
            pl.BlockSpec((tm, D_MODEL), lambda i: (i, 0)),
            pl.BlockSpec((None, None, D_MODEL, 2 * D_FF), lambda i: (layer, half, 0, 0)),
            pl.BlockSpec((None, None, D_FF, D_MODEL), lambda i: (layer, half, 0, 0)),
            pl.BlockSpec((1, D_MODEL), lambda i: (0, 0)),
            pl.BlockSpec((1, D_MODEL), lambda i: (0, 0)),
        ],
        out_specs=pl.BlockSpec((tm, D_MODEL), lambda i: (i, 0)),
        out_shape=jax.ShapeDtypeStruct((n, D_MODEL), F32),
        compiler_params=_params("parallel"),
        name="ffn",
    )(x, w_up, w_down, g, b)


def _outproj_t_body(y_ref, mt_ref, w_ref, g_ref, b_ref, o_ref, *, nb):
    for i in range(nb):
        rows = slice(i * MOBA_BLOCK, (i + 1) * MOBA_BLOCK)
        sub = lax.dot_general(mt_ref[i], w_ref[...], (((0,), (0,)), ((), ())),
                              preferred_element_type=F32)
        z = DN_ALPHA * y_ref[rows, :] + sub
        o_ref[rows, :] = _layer_norm(z, g_ref[...], b_ref[...])


def _outproj_t(y, mt, w_out, g, b, tm):
    n = y.shape[0]
    nb = tm // MOBA_BLOCK
    return pl.pallas_call(
        functools.partial(_outproj_t_body, nb=nb),
        grid=(n // tm,),
        in_specs=[
            pl.BlockSpec((tm, D_MODEL), lambda i: (i, 0)),
            pl.BlockSpec((nb, D_MODEL, MOBA_BLOCK), lambda i: (i, 0, 0)),
            pl.BlockSpec((D_MODEL, D_MODEL), lambda i: (0, 0)),
            pl.BlockSpec((1, D_MODEL), lambda i: (0, 0)),
            pl.BlockSpec((1, D_MODEL), lambda i: (0, 0)),
        ],
        out_specs=pl.BlockSpec((tm, D_MODEL), lambda i: (i, 0)),
        out_shape=jax.ShapeDtypeStruct((n, D_MODEL), F32),
        compiler_params=_params("parallel"),
        name="outproj_t",
    )(y, mt, w_out, g, b)


def _outproj_body(y_ref, m_ref, w_ref, g_ref, b_ref, o_ref):
    sub = _dot(m_ref[...].astype(BF16), w_ref[...])
    z = DN_ALPHA * y_ref[...] + sub
    o_ref[...] = _layer_norm(z, g_ref[...], b_ref[...])


def _outproj(y, m, w_out, g, b):
    n = y.shape[0]
    full = lambda shape: pl.BlockSpec(shape, lambda i: (0, 0))
    return pl.pallas_call(
        _outproj_body,
        grid=(1,),
        in_specs=[full((n, D_MODEL)), full((n, D_MODEL)), full((D_MODEL, D_MODEL)),
                  full((1, D_MODEL)), full((1, D_MODEL))],
        out_specs=full((n, D_MODEL)),
        out_shape=jax.ShapeDtypeStruct((n, D_MODEL), F32),
        compiler_params=_params("arbitrary"),
        name="outproj",
    )(y, m, w_out, g, b)


def _proj_common(y_ref, wk_ref, wkt_ref, wqt_ref, wvt_ref, k_ref, v_ref, kb_ref, qt_ref, vt_ref, nb):
    yb = y_ref[...].astype(BF16)
    k = _dot(yb, wk_ref[...])
    k_ref[0] = _dot_nt(wkt_ref[...], yb)
    kb = k.astype(BF16)
    qt = _dot_nt(wqt_ref[...], yb)
    vt = _dot_nt(wvt_ref[...], yb)
    v_ref[0] = vt
    for i in range(nb):
        cols = slice(i * MOBA_BLOCK, (i + 1) * MOBA_BLOCK)
        kb_ref[i] = kb[cols, :]
        qt_ref[i] = (qt[:, cols] * (QK_SCALE * LOG2E)).astype(BF16)
        vt_ref[i] = vt[:, cols].astype(BF16)
    return k, qt


def _proj_moba_body(y_ref, wk_ref, wkt_ref, wqt_ref, wvt_ref, hm_ref,
                    k_ref, v_ref, kb_ref, qt_ref, vt_ref, sel_ref, km_ref, *, nb, n_blk):
    t = pl.program_id(1)

    @pl.when(t == 0)
    def _():
        km_ref[...] = jnp.zeros_like(km_ref)

    k, qt = _proj_common(y_ref, wk_ref, wkt_ref, wqt_ref, wvt_ref,
                         k_ref, v_ref, kb_ref, qt_ref, vt_ref, nb)
    for i in range(nb):
        rows = slice(i * MOBA_BLOCK, (i + 1) * MOBA_BLOCK)
        km_ref[pl.ds(t * nb + i, 1), :] = jnp.sum(k[rows, :], axis=0, keepdims=True) / MOBA_BLOCK

    km = km_ref[...]
    hm = hm_ref[...]
    kmb = jnp.concatenate(
        [jnp.broadcast_to(km[n:n + 1, :], (N_HEADS, D_MODEL)) * hm for n in range(n_blk)], axis=0)
    a_hi = kmb.astype(BF16)
    a_mid = (kmb - a_hi.astype(F32)).astype(BF16)
    q_hi = qt.astype(BF16)
    q_mid = (qt - q_hi.astype(F32)).astype(BF16)
    gate = _dot(a_hi, q_hi) + _dot(a_hi, q_mid) + _dot(a_mid, q_hi)

    tm = nb * MOBA_BLOCK
    col = lax.broadcasted_iota(jnp.int32, (N_HEADS, tm), 1)
    q_blk = t * nb + col // MOBA_BLOCK
    g = []
    for n in range(n_blk):
        g.append(jnp.where(n < q_blk, gate[n * N_HEADS:(n + 1) * N_HEADS, :], NEG))
    bias = []
    for n in range(n_blk):
        rank = jnp.zeros((N_HEADS, tm), jnp.int32)
        for m in range(n_blk):
            if m == n:
                continue
            ahead = (g[m] >= g[n]) if m < n else (g[m] > g[n])
            rank = rank + ahead.astype(jnp.int32)
        keep = (n < q_blk) & (rank < MOBA_TOPK)
        bias.append(jnp.where(keep, 0.0, NEG))
    bias = jnp.concatenate(bias, axis=0)
    for i in range(nb):
        sel_ref[i] = bias[:, i * MOBA_BLOCK:(i + 1) * MOBA_BLOCK]


def _proj_moba(y, wk, wkt, wqt, wvt, hm, batch, seq, tm):
    n = y.shape[0]
    nb = tm // MOBA_BLOCK
    n_blk = seq // MOBA_BLOCK
    tiles = seq // tm
    full = lambda shape: pl.BlockSpec(shape, lambda b, t: (0,) * len(shape))
    row = lambda b, t: (b * tiles + t, 0)
    blk = lambda b, t: (b * tiles + t, 0, 0)
    tcol = lambda b, t: (b, 0, t)
    nblocks = n // MOBA_BLOCK
    return pl.pallas_call(
        functools.partial(_proj_moba_body, nb=nb, n_blk=n_blk),
        grid=(batch, tiles),
        in_specs=[pl.BlockSpec((tm, D_MODEL), row), full((D_MODEL, D_MODEL)), full((D_MODEL, D_MODEL)),
                  full((D_MODEL, D_MODEL)), full((D_MODEL, D_MODEL)), full((N_HEADS, D_MODEL))],
        out_specs=[
            pl.BlockSpec((1, D_MODEL, tm), tcol),
            pl.BlockSpec((1, D_MODEL, tm), tcol),
            pl.BlockSpec((nb, MOBA_BLOCK, D_MODEL), blk),
            pl.BlockSpec((nb, D_MODEL, MOBA_BLOCK), blk),
            pl.BlockSpec((nb, D_MODEL, MOBA_BLOCK), blk),
            pl.BlockSpec((nb, n_blk * N_HEADS, MOBA_BLOCK), blk),
        ],
        out_shape=[
            jax.ShapeDtypeStruct((batch, D_MODEL, seq), F32),
            jax.ShapeDtypeStruct((batch, D_MODEL, seq), F32),
            jax.ShapeDtypeStruct((nblocks, MOBA_BLOCK, D_MODEL), BF16),
            jax.ShapeDtypeStruct((nblocks, D_MODEL, MOBA_BLOCK), BF16),
            jax.ShapeDtypeStruct((nblocks, D_MODEL, MOBA_BLOCK), BF16),
            jax.ShapeDtypeStruct((nblocks, n_blk * N_HEADS, MOBA_BLOCK), F32),
        ],
        scratch_shapes=[pltpu.VMEM((n_blk, D_MODEL), F32)],
        compiler_params=_params("arbitrary", "arbitrary"),
        name="proj_moba",
    )(y, wk, wkt, wqt, wvt, hm)


def _logf_from(yb, wf_ref, bf_ref, tm):
    lane = lax.broadcasted_iota(jnp.int32, (tm, LANES), 1)
    zf = _dot(yb, wf_ref[...]) + bf_ref[...]
    return jnp.where(lane < N_HEADS, _log_sigmoid(zf), 0.0)


def _pack3(x):
    hi, mid, lo = _split3(x)
    packed = (hi.astype(F32) + pltpu.roll(mid.astype(F32), N_HEADS, 1)
              + pltpu.roll(lo.astype(F32), 2 * N_HEADS, 1))
    return packed.astype(BF16)


def _proj_fox_body(y_ref, wk_ref, wkt_ref, wqt_ref, wvt_ref, wf_ref, bf_ref, tri_ref, place_ref,
                   k_ref, v_ref, kb_ref, qt_ref, vt_ref, lf_ref, ck_ref, carry_ref, *, nb):
    t = pl.program_id(1)

    @pl.when(t == 0)
    def _():
        carry_ref[...] = jnp.zeros_like(carry_ref)

    _proj_common(y_ref, wk_ref, wkt_ref, wqt_ref, wvt_ref, k_ref, v_ref, kb_ref, qt_ref, vt_ref, nb)
    tm = nb * MOBA_BLOCK
    yb = y_ref[...].astype(BF16)
    lf = _logf_from(yb, wf_ref, bf_ref, tm)
    lf_ref[...] = lf[:, :N_HEADS]
    ct = _dot_exact_lhs(tri_ref[...], lf) + carry_ref[0:1, :]
    carry_ref[0:1, :] = ct[tm - 1:tm, :]
    ck = _dot(_pack3(ct), place_ref[...]).astype(BF16)
    for i in range(nb):
        ck_ref[i] = ck[i * MOBA_BLOCK:(i + 1) * MOBA_BLOCK, :]


def _proj_fox(y, wk, wkt, wqt, wvt, wf, bf, tri, place, batch, seq, tm):
    n = y.shape[0]
    nb = tm // MOBA_BLOCK
    tiles = seq // tm
    full = lambda shape: pl.BlockSpec(shape, lambda b, t: (0,) * len(shape))
    row = lambda b, t: (b * tiles + t, 0)
    blk = lambda b, t: (b * tiles + t, 0, 0)
    tcol = lambda b, t: (b, 0, t)
    nblocks = n // MOBA_BLOCK
    return pl.pallas_call(
        functools.partial(_proj_fox_body, nb=nb),
        grid=(batch, tiles),
        in_specs=[pl.BlockSpec((tm, D_MODEL), row), full((D_MODEL, D_MODEL)), full((D_MODEL, D_MODEL)),
                  full((D_MODEL, D_MODEL)), full((D_MODEL, D_MODEL)), full((D_MODEL, LANES)),
                  full((1, LANES)), full((tm, tm)), full((LANES, D_MODEL))],
        out_specs=[
            pl.BlockSpec((1, D_MODEL, tm), tcol),
            pl.BlockSpec((1, D_MODEL, tm), tcol),
            pl.BlockSpec((nb, MOBA_BLOCK, D_MODEL), blk),
            pl.BlockSpec((nb, D_MODEL, MOBA_BLOCK), blk),
            pl.BlockSpec((nb, D_MODEL, MOBA_BLOCK), blk),
            pl.BlockSpec((tm, N_HEADS), row),
            pl.BlockSpec((nb, MOBA_BLOCK, D_MODEL), blk),
        ],
        out_shape=[
            jax.ShapeDtypeStruct((batch, D_MODEL, seq), F32),
            jax.ShapeDtypeStruct((batch, D_MODEL, seq), F32),
            jax.ShapeDtypeStruct((nblocks, MOBA_BLOCK, D_MODEL), BF16),
            jax.ShapeDtypeStruct((nblocks, D_MODEL, MOBA_BLOCK), BF16),
            jax.ShapeDtypeStruct((nblocks, D_MODEL, MOBA_BLOCK), BF16),
            jax.ShapeDtypeStruct((n, N_HEADS), F32),
            jax.ShapeDtypeStruct((nblocks, MOBA_BLOCK, D_MODEL), BF16),
        ],
        scratch_shapes=[pltpu.VMEM((8, LANES), F32)],
        compiler_params=_params("arbitrary", "arbitrary"),
        name="proj_fox",
    )(y, wk, wkt, wqt, wvt, wf, bf, tri, place)


def _proj_nat_body(y_ref, wq_ref, wk_ref, wv_ref, *rest, with_logf):
    if with_logf:
        wf_ref, bf_ref, q_ref, k_ref, v_ref, lf_ref = rest
    else:
        q_ref, k_ref, v_ref = rest
    yb = y_ref[...].astype(BF16)
    q_ref[...] = _dot(yb, wq_ref[...])
    k_ref[...] = _dot(yb, wk_ref[...])
    v_ref[...] = _dot(yb, wv_ref[...])
    if with_logf:
        lf_ref[...] = _logf_from(yb, wf_ref, bf_ref, y_ref.shape[0])[:, :N_HEADS]


def _proj_nat(y, wq, wk, wv, wf=None, bf=None):
    n = y.shape[0]
    with_logf = wf is not None
    full = lambda shape: pl.BlockSpec(shape, lambda i: (0,) * len(shape))
    ins = [y, wq, wk, wv]
    in_specs = [full((n, D_MODEL))] + [full((D_MODEL, D_MODEL))] * 3
    out_specs = [full((n, D_MODEL))] * 3
    out_shape = [jax.ShapeDtypeStruct((n, D_MODEL), F32)] * 3
    if with_logf:
        ins += [wf, bf]
        in_specs += [full((D_MODEL, LANES)), full((1, LANES))]
        out_specs += [full((n, N_HEADS))]
        out_shape += [jax.ShapeDtypeStruct((n, N_HEADS), F32)]
    return pl.pallas_call(
        functools.partial(_proj_nat_body, with_logf=with_logf),
        grid=(1,),
        in_specs=in_specs,
        out_specs=out_specs,
        out_shape=out_shape,
        compiler_params=_params("arbitrary"),
        name="proj_nat",
    )(*ins)


def _attn_body(kb_ref, aux_ref, qt_ref, vt_ref, qa_ref, *rest, n_blk, use_sel, sb):
    if use_sel:
        sel_ref, o_ref, s_ref = rest
    else:
        o_ref, s_ref = rest
    hp = pl.program_id(1)
    tile = sb * MOBA_BLOCK
    key_i = lax.broadcasted_iota(jnp.int32, (tile, tile), 0)
    qry_i = lax.broadcasted_iota(jnp.int32, (tile, tile), 1)
    causal = key_i <= qry_i
    rid = lax.broadcasted_iota(jnp.int32, (PAIR, tile), 0)
    hrows = [slice(i * HEAD_DIM, (i + 1) * HEAD_DIM) for i in range(2)]

    def cat(parts, axis):
        return parts[0] if len(parts) == 1 else jnp.concatenate(parts, axis=axis)

    def keys_of(t):
        kb = cat([kb_ref[t * sb + a] for a in range(sb)], 0)
        aux = cat([aux_ref[t * sb + a] for a in range(sb)], 0)
        return jnp.concatenate([kb, aux], axis=1)

    def vt_of(t, i):
        return cat([vt_ref[t * sb + a, hrows[i], :] for a in range(sb)], 1)

    def sel_rows(jt, kt, i, a, first_c):
        row = (kt * sb + a) * N_HEADS + 2 * hp + i
        parts = [sel_ref[jt * sb + c, pl.ds(row, 1), :] if c >= first_c
                 else jnp.zeros((1, MOBA_BLOCK), F32) for c in range(sb)]
        return cat(parts, 1)

    def add_sel(s, jt, kt, i, diag):
        if not use_sel:
            return s
        rows = []
        for a in range(sb):
            sa = s[a * MOBA_BLOCK:(a + 1) * MOBA_BLOCK, :]
            if diag and a == sb - 1:
                rows.append(sa)
            else:
                rows.append(sa + sel_rows(jt, kt, i, a, a + 1 if diag else 0))
        return cat(rows, 0)

    def q_tile(jt, carry):
        qt = cat([qt_ref[jt * sb + c] for c in range(sb)], 1)
        qaug = []
        for i in range(2):
            own = (rid >= i * HEAD_DIM) & (rid < (i + 1) * HEAD_DIM)
            qa = cat([qa_ref[i]] * sb, 1)
            qaug.append(jnp.concatenate([jnp.where(own, qt, jnp.zeros_like(qt)), qa], axis=0))

        def produce(kt, slot):
            kcat = keys_of(kt)
            for i in range(2):
                s_ref[slot, i] = _dot(kcat, qaug[i])

        def consume(c, kt, slot, diag):
            new = []
            for i in range(2):
                m, l, acc = c[3 * i:3 * i + 3]
                s = add_sel(s_ref[slot, i], jt, kt, i, diag)
                if diag:
                    s = jnp.where(causal, s, NEG)
                m_new = jnp.maximum(m, jnp.max(s, axis=0, keepdims=True))
                corr = jnp.exp2(m - m_new)
                p = jnp.exp2(s - m_new)
                l = l * corr + jnp.sum(p, axis=0, keepdims=True)
                acc = acc * corr + _dot(vt_of(kt, i), p.astype(BF16))
                new += [m_new, l, acc]
            return tuple(new)

        produce(0, 0)
        state = []
        for i in range(2):
            state += [jnp.full((1, tile), NEG, F32), jnp.zeros((1, tile), F32), jnp.zeros((HEAD_DIM, tile), F32)]

        def past(kt, c):
            slot = kt % 2
            c = consume(c, kt, slot, False)
            produce(kt + 1, 1 - slot)
            return c

        state = lax.fori_loop(0, jt, past, tuple(state))
        state = consume(state, jt, jt % 2, True)
        for i in range(2):
            m, l, acc = state[3 * i:3 * i + 3]
            out = (acc / l).astype(BF16)
            for c in range(sb):
                o_ref[jt * sb + c, hrows[i], :] = out[:, c * MOBA_BLOCK:(c + 1) * MOBA_BLOCK]
        return carry

    lax.fori_loop(0, n_blk // sb, q_tile, 0)


def _attn(kb, aux, qt, vt, qa, sel, batch, seq, aux_is_const):
    n_blk = seq // MOBA_BLOCK
    nblocks = kb.shape[0]
    use_sel = sel is not None
    aux_map = (lambda b, hp: (0, 0, 0)) if aux_is_const else (lambda b, hp: (b, 0, hp))
    in_specs = [
        pl.BlockSpec((n_blk, MOBA_BLOCK, PAIR), lambda b, hp: (b, 0, hp)),
        pl.BlockSpec((n_blk, MOBA_BLOCK, PAIR), aux_map),
        pl.BlockSpec((n_blk, PAIR, MOBA_BLOCK), lambda b, hp: (b, hp, 0)),
        pl.BlockSpec((n_blk, PAIR, MOBA_BLOCK), lambda b, hp: (b, hp, 0)),
        pl.BlockSpec((2, PAIR, MOBA_BLOCK), lambda b, hp: (hp, 0, 0)),
    ]
    ins = [kb, aux, qt, vt, qa]
    if use_sel:
        in_specs.append(pl.BlockSpec((n_blk, n_blk * N_HEADS, MOBA_BLOCK), lambda b, hp: (b, 0, 0)))
        ins.append(sel)
    return pl.pallas_call(
        functools.partial(_attn_body, n_blk=n_blk, use_sel=use_sel, sb=ATTN_BLOCKS_PER_TILE),
        grid=(batch, N_PAIRS),
        in_specs=in_specs,
        out_specs=pl.BlockSpec((n_blk, PAIR, MOBA_BLOCK), lambda b, hp: (b, hp, 0)),
        out_shape=jax.ShapeDtypeStruct((nblocks, D_MODEL, MOBA_BLOCK), BF16),
        scratch_shapes=[pltpu.VMEM((2, 2, ATTN_BLOCKS_PER_TILE * MOBA_BLOCK,
                                    ATTN_BLOCKS_PER_TILE * MOBA_BLOCK), F32)],
        compiler_params=_params("parallel", "parallel"),
        name="attn_moba" if use_sel else "attn_fox",
    )(*ins)


def _diag_rows(x, t):
    return jnp.sum(x.reshape(N_HEADS, t, D_MODEL), axis=0)


def _dot_exact_rhs(x, b_bf):
    hi, mid, lo = _split3(x)
    return _dot(hi, b_bf) + _dot(mid, b_bf) + _dot(lo, b_bf)


def _q_rows(q_ref, dm, scale):
    return jnp.concatenate([q_ref[...]] * N_HEADS, axis=0) * (dm * scale)


def _new_token_scores(qbt, kn_ref, vn_ref, nbias, extra_cols, t_new):
    pad = jnp.zeros((LANES - t_new, D_MODEL), F32)
    kn = jnp.concatenate([kn_ref[...], pad], axis=0).astype(BF16)
    vn = jnp.concatenate([vn_ref[...], pad], axis=0).astype(BF16)
    if extra_cols is not None:
        kn = jnp.concatenate([kn, extra_cols], axis=1)
    return _dot_nt(qbt, kn) + nbias, vn


def _moba_sample_body(pt_ref, q_ref, kn_ref, vn_ref, dm_ref, slb_ref, qp_ref, nbias_ref,
                      *rest, pp, t_new, n_blk):
    del pt_ref
    k_refs, v_refs = rest[:pp], rest[pp:2 * pp]
    o_ref, qbt_ref, qmid_ref, g_ref, m_ref, l_ref, od_ref = rest[2 * pp:]
    s = pl.program_id(1)
    rows = N_HEADS * t_new
    dm = dm_ref[...]
    lane = lax.broadcasted_iota(jnp.int32, (rows, LANES), 1)

    @pl.when(s == 0)
    def _():
        q16 = _q_rows(q_ref, dm, QK_SCALE)
        hi = q16.astype(BF16)
        qbt_ref[...] = hi
        qmid_ref[...] = (q16 - hi.astype(F32)).astype(BF16)
        g_ref[...] = jnp.zeros_like(g_ref)
        m_ref[...] = jnp.zeros_like(m_ref)
        l_ref[...] = jnp.zeros_like(l_ref)

    qbt = qbt_ref[...]
    per_blk = MOBA_BLOCK // PAGE_SIZE
    pair_rows = 2 * t_new
    first_head = lax.broadcasted_iota(jnp.int32, (t_new, LANES), 1) < HEAD_DIM

    def pair_q(ref, g):
        return ref[g * pair_rows:(g + 1) * pair_rows, g * PAIR:(g + 1) * PAIR]

    g_new = m_new = l_new = jnp.zeros((rows, LANES), F32)
    for i in range(pp // per_blk):
        n = s * (pp // per_blk) + i
        kf = jnp.concatenate([k_refs[per_blk * i + j][...] for j in range(per_blk)], axis=1)
        kt = kf.astype(BF16)
        vt = jnp.concatenate([v_refs[per_blk * i + j][...] for j in range(per_blk)], axis=1).astype(BF16)
        ks = kf[:, :LANES] + kf[:, LANES:]
        ks_hi = ks.astype(BF16)
        ks_mid = (ks - ks_hi.astype(F32)).astype(BF16)
        sc, gsum = [], []
        for g in range(N_PAIRS):
            feat = slice(g * PAIR, (g + 1) * PAIR)
            qg, qg_mid = pair_q(qbt_ref, g), pair_q(qmid_ref, g)
            sc.append(_dot(qg, kt[feat, :]))
            gsum.append(jnp.sum(_dot(qg, ks_hi[feat, :]), axis=1, keepdims=True)
                        + jnp.sum(_dot(qg_mid, ks_hi[feat, :]), axis=1, keepdims=True)
                        + jnp.sum(_dot(qg, ks_mid[feat, :]), axis=1, keepdims=True))
        sc = jnp.concatenate(sc, axis=0)
        gsum = jnp.concatenate(gsum, axis=0)
        dist = qp_ref[...] - (n * MOBA_BLOCK).astype(F32)
        sc = sc - slb_ref[...] * dist
        m = jnp.max(sc, axis=1, keepdims=True)
        e = jnp.exp(sc - m)
        l = jnp.sum(e, axis=1, keepdims=True)
        eb = e.astype(BF16)
        od_ref[n] = jnp.concatenate(
            [_dot_nt(eb[g * pair_rows:(g + 1) * pair_rows, :], vt[g * PAIR:(g + 1) * PAIR, :])
             for g in range(N_PAIRS)], axis=0)
        onehot = (lane == n).astype(F32)
        g_new = g_new + gsum * onehot
        m_new = m_new + m * onehot
        l_new = l_new + l * onehot
    g_ref[...] += g_new
    m_ref[...] += m_new
    l_ref[...] += l_new

    @pl.when(s == pl.num_programs(1) - 1)
    def _():
        low = jnp.float32(-3.0e38)
        g = jnp.where(lane < n_blk, g_ref[...], low)
        lane_f = lane.astype(F32)
        sel = lane < 0
        for _ in range(MOBA_TOPK):
            mx = jnp.max(g, axis=1, keepdims=True)
            first = jnp.min(jnp.where(g == mx, lane_f, float(LANES)), axis=1, keepdims=True)
            pick = lane_f == first
            sel = sel | pick
            g = jnp.where(pick, low, g)
        sn, vn = _new_token_scores(qbt, kn_ref, vn_ref, nbias_ref[...], None, t_new)
        m_n = jnp.max(sn, axis=1, keepdims=True)
        e_n = jnp.exp(sn - m_n)
        l_n = jnp.sum(e_n, axis=1, keepdims=True)
        o_n = _dot(e_n.astype(BF16), vn)
        od_n = jnp.concatenate([o_n[g * pair_rows:(g + 1) * pair_rows, g * PAIR:(g + 1) * PAIR]
                                for g in range(N_PAIRS)], axis=0)
        m_all = m_ref[...]
        m_tot = jnp.maximum(jnp.max(jnp.where(sel, m_all, NEG), axis=1, keepdims=True), m_n)
        w = jnp.where(sel, jnp.exp(jnp.minimum(m_all - m_tot, 0.0)), 0.0)
        w_n = jnp.exp(m_n - m_tot)
        l_tot = jnp.sum(w * l_ref[...], axis=1, keepdims=True) + w_n * l_n
        acc = w_n * od_n
        for n in range(n_blk):
            acc = acc + w[:, n:n + 1] * od_ref[n]
        out = acc / l_tot
        o_ref[...] = jnp.concatenate(
            [jnp.where(first_head, out[g * pair_rows:g * pair_rows + t_new, :],
                       out[g * pair_rows + t_new:(g + 1) * pair_rows, :]) for g in range(N_PAIRS)], axis=1)


def _page_spec(i, pp, n_pages, reverse):
    def index(b, s, pt):
        p = s * pp + i
        if reverse:
            p = n_pages - 1 - p
        return (pt[b * n_pages + p], 0, 0)
    return index


def _moba_sample(q, k_new, v_new, cache_kt, cache_vt, page_table, consts, t_new):
    batch = q.shape[0] // t_new
    n_pages = page_table.shape[1]
    n_blk = n_pages * PAGE_SIZE // MOBA_BLOCK
    pp = PAGES_PER_STEP
    rows = N_HEADS * t_new
    tok = pl.BlockSpec((t_new, D_MODEL), lambda b, s, pt: (b, 0))
    const = lambda a: pl.BlockSpec(a.shape, lambda b, s, pt: (0, 0))
    page = lambda i: pl.BlockSpec((None, D_MODEL, PAGE_SIZE), _page_spec(i, pp, n_pages, False))
    grid_spec = pltpu.PrefetchScalarGridSpec(
        num_scalar_prefetch=1,
        grid=(batch, n_pages // pp),
        in_specs=[tok, tok, tok] + [const(a) for a in consts]
        + [page(i) for i in range(pp)] + [page(i) for i in range(pp)],
        out_specs=tok,
        scratch_shapes=[
            pltpu.VMEM((rows, D_MODEL), BF16),
            pltpu.VMEM((rows, D_MODEL), BF16),
            pltpu.VMEM((rows, LANES), F32),
            pltpu.VMEM((rows, LANES), F32),
            pltpu.VMEM((rows, LANES), F32),
            pltpu.VMEM((n_blk, rows, LANES), F32),
        ],
    )
    return pl.pallas_call(
        functools.partial(_moba_sample_body, pp=pp, t_new=t_new, n_blk=n_blk),
        grid_spec=grid_spec,
        out_shape=jax.ShapeDtypeStruct((batch * t_new, D_MODEL), F32),
        compiler_params=_params("parallel", "arbitrary"),
        name="moba_sample",
    )(page_table.reshape(-1), q, k_new, v_new, *consts,
      *([cache_kt] * pp), *([cache_vt] * pp))


def _fox_sample_body(pt_ref, q_ref, kn_ref, vn_ref, lfn_ref, dm_ref, qaux_ref, tri_ref, nbias_ref,
                     later_ref, ones_ref, *rest, pp, t_new):
    del pt_ref
    k_refs, v_refs, lf_refs = rest[:pp], rest[pp:2 * pp], rest[2 * pp:3 * pp]
    o_ref, qbt_ref, pad_ref, carry_ref, m_ref, l_ref, acc_ref = rest[3 * pp:]
    s = pl.program_id(1)
    dm = dm_ref[...]

    @pl.when(s == 0)
    def _():
        qbt_ref[:, :D_MODEL] = _q_rows(q_ref, dm, QK_SCALE).astype(BF16)
        qbt_ref[:, D_MODEL:] = qaux_ref[...]
        carry_ref[...] = jnp.zeros_like(carry_ref)
        m_ref[...] = jnp.full_like(m_ref, NEG)
        l_ref[...] = jnp.zeros_like(l_ref)
        acc_ref[...] = jnp.zeros_like(acc_ref)

    qbt = qbt_ref[...]

    def online(sc, weigh):
        m_old = m_ref[:, 0:1]
        m_new = jnp.maximum(m_old, jnp.max(sc, axis=1, keepdims=True))
        corr = jnp.exp(m_old - m_new)
        e = jnp.exp(sc - m_new)
        l_ref[...] = jnp.broadcast_to(l_ref[:, 0:1] * corr + jnp.sum(e, axis=1, keepdims=True), l_ref.shape)
        acc_ref[...] = acc_ref[...] * corr + weigh(e.astype(BF16))
        m_ref[...] = jnp.broadcast_to(m_new, m_ref.shape)

    zrows = jnp.zeros((LANES - 3 * N_HEADS, PAGE_SIZE), BF16)
    carry = carry_ref[...]
    kcols = []
    for i in range(pp):
        lft = lf_refs[i][...]
        suf = _dot_exact_rhs(lft, later_ref[...]) + carry
        carry = carry + _dot_exact_rhs(lft, ones_ref[...])
        hi, mid, lo = _split3(suf)
        kcols.append(jnp.concatenate([k_refs[i][...].astype(BF16), hi, mid, lo, zrows], axis=0))
    carry_ref[...] = carry
    kcat = jnp.concatenate(kcols, axis=1)
    vt = jnp.concatenate([v_refs[i][...].astype(BF16) for i in range(pp)], axis=1)
    online(_dot(qbt, kcat), lambda e: _dot_nt(e, vt))

    @pl.when(s == pl.num_programs(1) - 1)
    def _():
        pad_ref[...] = jnp.zeros_like(pad_ref)
        pad_ref[0:t_new, :N_HEADS] = lfn_ref[...]
        c_new = _dot_exact_lhs(tri_ref[...], pad_ref[...])
        sn, vn = _new_token_scores(qbt, kn_ref, vn_ref, nbias_ref[...], _pack3(-c_new), t_new)
        online(sn, lambda e: _dot(e, vn))
        out = acc_ref[...] / l_ref[:, 0:1]
        o_ref[...] = _diag_rows(out * dm, t_new)


def _fox_sample(q, k_new, v_new, lf_new, cache_kt, cache_vt, cache_lft, page_table, consts, t_new):
    batch = q.shape[0] // t_new
    n_pages = page_table.shape[1]
    pp = PAGES_PER_STEP
    rows = N_HEADS * t_new
    tok = pl.BlockSpec((t_new, D_MODEL), lambda b, s, pt: (b, 0))
    const = lambda a: pl.BlockSpec(a.shape, lambda b, s, pt: (0, 0))
    page = lambda i: pl.BlockSpec((None, D_MODEL, PAGE_SIZE), _page_spec(i, pp, n_pages, True))
    lfpage = lambda i: pl.BlockSpec((None, N_HEADS, PAGE_SIZE), _page_spec(i, pp, n_pages, True))
    grid_spec = pltpu.PrefetchScalarGridSpec(
        num_scalar_prefetch=1,
        grid=(batch, n_pages // pp),
        in_specs=[tok, tok, tok, pl.BlockSpec((t_new, N_HEADS), lambda b, s, pt: (b, 0))]
        + [const(a) for a in consts]
        + [page(i) for i in range(pp)] + [page(i) for i in range(pp)] + [lfpage(i) for i in range(pp)],
        out_specs=tok,
        scratch_shapes=[
            pltpu.VMEM((rows, D_MODEL + LANES), BF16),
            pltpu.VMEM((LANES, LANES), F32),
            pltpu.VMEM((N_HEADS, PAGE_SIZE), F32),
            pltpu.VMEM((rows, LANES), F32),
            pltpu.VMEM((rows, LANES), F32),
            pltpu.VMEM((rows, D_MODEL), F32),
        ],
    )
    return pl.pallas_call(
        functools.partial(_fox_sample_body, pp=pp, t_new=t_new),
        grid_spec=grid_spec,
        out_shape=jax.ShapeDtypeStruct((batch * t_new, D_MODEL), F32),
        compiler_params=_params("parallel", "arbitrary"),
        name="fox_sample",
    )(page_table.reshape(-1), q, k_new, v_new, lf_new, *consts,
      *([cache_kt] * pp), *([cache_vt] * pp), *([cache_lft] * pp))


def _np_split3(x):
    x = jnp.asarray(x, F32)
    hi, mid, lo = _split3(x)
    return hi, mid, lo


def _alibi_slopes():
    return 2.0 ** (-8.0 * np.arange(1, N_HEADS + 1) / N_HEADS)


def _prompt_tables(seq, tm):
    n_blk = seq // MOBA_BLOCK
    pos = np.zeros((n_blk, MOBA_BLOCK, PAIR), np.float32)
    pos[:, :, 0:3] = np.arange(MOBA_BLOCK, dtype=np.float32)[None, :, None]
    pos[:, :, 3:6] = (np.arange(n_blk, dtype=np.float32) * MOBA_BLOCK)[:, None, None]
    s_hi, s_mid, s_lo = _np_split3(_alibi_slopes() * LOG2E)
    parts = jnp.stack([s_hi, s_mid, s_lo, s_hi, s_mid, s_lo], axis=1).astype(F32)
    qa_moba = jnp.zeros((N_HEADS, PAIR, MOBA_BLOCK), F32)
    qa_moba = qa_moba.at[:, 0:6, :].set(jnp.broadcast_to(parts[:, :, None], (N_HEADS, 6, MOBA_BLOCK)))
    l_hi = np.float32(LOG2E).astype(BF16).astype(np.float32)
    l_mid = np.float32(LOG2E - l_hi).astype(BF16).astype(np.float32)
    l_lo = np.float32(LOG2E - l_hi - l_mid).astype(BF16).astype(np.float32)
    l_parts = (l_hi, l_mid, l_lo)
    qa_fox = np.zeros((N_HEADS, PAIR, MOBA_BLOCK), np.float32)
    place = np.zeros((LANES, D_MODEL), np.float32)
    for h in range(N_HEADS):
        base = 9 * (h % 2)
        for piece in range(3):
            for r in range(3):
                qa_fox[h, base + 3 * piece + r, :] = -l_parts[r]
                place[piece * N_HEADS + h, (h // 2) * PAIR + base + 3 * piece + r] = 1.0
    tri = np.tril(np.ones((tm, tm), np.float32))
    hm = (np.arange(D_MODEL)[None, :] // HEAD_DIM == np.arange(N_HEADS)[:, None]).astype(np.float32)
    return dict(pos=jnp.asarray(pos, BF16), qa_moba=qa_moba.astype(BF16), qa_fox=jnp.asarray(qa_fox, BF16),
                place=jnp.asarray(place, BF16), tri=jnp.asarray(tri, BF16), hm=jnp.asarray(hm, F32))


def _sample_tables(t_new, past_len):
    rows = N_HEADS * t_new
    h_of = np.arange(rows) // t_new
    t_of = np.arange(rows) % t_new
    dm = (np.arange(D_MODEL)[None, :] // HEAD_DIM == h_of[:, None]).astype(np.float32)
    slopes = _alibi_slopes().astype(np.float32)
    slb = np.broadcast_to(slopes[h_of][:, None], (rows, MOBA_BLOCK)).astype(np.float32)
    qp = (past_len + t_of[:, None] - np.arange(MOBA_BLOCK)[None, :]).astype(np.float32)
    s_idx = np.arange(LANES)[None, :]
    visible = (s_idx <= t_of[:, None]) & (s_idx < t_new)
    nb_moba = np.where(visible, -slopes[h_of][:, None] * (t_of[:, None] - s_idx), NEG).astype(np.float32)
    nb_fox = np.where(visible, 0.0, NEG).astype(np.float32)
    qaux = np.zeros((rows, LANES), np.float32)
    for piece in range(3):
        qaux[np.arange(rows), piece * N_HEADS + h_of] = 1.0
    tri = np.tril(np.ones((LANES, LANES), np.float32))
    later = np.tril(np.ones((PAGE_SIZE, PAGE_SIZE), np.float32), k=-1)
    ones = np.ones((PAGE_SIZE, PAGE_SIZE), np.float32)
    bf = lambda a: jnp.asarray(a, BF16)
    f32 = lambda a: jnp.asarray(a, F32)
    return dict(
        moba=(f32(dm), f32(slb), f32(qp), f32(nb_moba)),
        fox=(f32(dm), bf(qaux), bf(tri), f32(nb_fox), bf(later), bf(ones)))


def kernel(x_prompt, x_sample, cache_k_moba, cache_v_moba, cache_k_fox, cache_v_fox, cache_logf_fox,
           page_table, w_in_moba, w_out_moba, w_in_fox, b_f_fox, w_out_fox, w_ffn_up, w_ffn_down,
           ln_g, ln_b):
    batch, seq, d = x_prompt.shape
    dec_batch, t_new, _ = x_sample.shape
    n_pool = cache_k_moba.shape[1]
    past_len = page_table.shape[1] * PAGE_SIZE
    assert d == D_MODEL and seq % 512 == 0 and t_new == 8
    tm = 512
    pt = _prompt_tables(seq, tm)
    st = _sample_tables(t_new, past_len)

    yp = x_prompt.reshape(batch * seq, d)
    ys = x_sample.reshape(dec_batch * t_new, d)
    n_s = ys.shape[0]

    def norm(i, j):
        return ln_g[i, j].reshape(1, d), ln_b[i, j].reshape(1, d)

    w_up_bf = w_ffn_up.astype(BF16)
    w_down_bf = w_ffn_down.astype(BF16)

    def ffn_both(yp, ys, i, j):
        g, b = norm(i, 2 * j)
        return (_ffn(yp, w_up_bf, w_down_bf, i, j, g, b, FFN_TOKENS),
                _ffn(ys, w_up_bf, w_down_bf, i, j, g, b, n_s))

    def heads(z, lead):
        return z.reshape(lead + (N_HEADS, HEAD_DIM))[None]

    def heads_t(zt):
        return jnp.transpose(zt.reshape(batch, N_HEADS, HEAD_DIM, seq), (0, 3, 1, 2))[None]

    def pages_t(cache):
        return jnp.transpose(cache, (0, 2, 3, 1)).reshape(n_pool, d, PAGE_SIZE)

    outs = {}
    for i in range(DEPTH):
        li = i // 2
        yp, ys = ffn_both(yp, ys, i, 0)
        if i % 2 == 0:
            w = w_in_moba[li].astype(BF16)
            wq, wk, wv = w[:, :d], w[:, d:2 * d], w[:, 2 * d:3 * d]
            kp, vp, kb, qt, vt, sel = _proj_moba(yp, wk, wk.T, wq.T, wv.T, pt["hm"], batch, seq, tm)
            qs, ks, vs = _proj_nat(ys, wq, wk, wv)
            mp = _attn(kb, pt["pos"], qt, vt, pt["qa_moba"], sel, batch, seq, True)
            ms = _moba_sample(qs, ks, vs, pages_t(cache_k_moba[li]), pages_t(cache_v_moba[li]),
                              page_table, st["moba"], t_new)
            w_out = w_out_moba[li].astype(BF16)
            outs["kmp"], outs["vmp"] = heads_t(kp), heads_t(vp)
            outs["kms"], outs["vms"] = heads(ks, (dec_batch, t_new)), heads(vs, (dec_batch, t_new))
        else:
            w = w_in_fox[li].astype(BF16)
            wq, wk, wv = w[:, :d], w[:, d:2 * d], w[:, 2 * d:3 * d]
            wf = jnp.pad(w[:, 3 * d:], ((0, 0), (0, LANES - N_HEADS)))
            bf = jnp.pad(b_f_fox[li].astype(F32).reshape(1, N_HEADS), ((0, 0), (0, LANES - N_HEADS)))
            kp, vp, kb, qt, vt, lfp, ck = _proj_fox(yp, wk, wk.T, wq.T, wv.T, wf, bf, pt["tri"], pt["place"],
                                                   batch, seq, tm)
            qs, ks, vs, lfs = _proj_nat(ys, wq, wk, wv, wf, bf)
            mp = _attn(kb, ck, qt, vt, pt["qa_fox"], None, batch, seq, False)
            ms = _fox_sample(qs, ks, vs, lfs, pages_t(cache_k_fox[li]), pages_t(cache_v_fox[li]),
                             jnp.transpose(cache_logf_fox[li], (0, 2, 1)), page_table, st["fox"], t_new)
            w_out = w_out_fox[li].astype(BF16)
            outs["kfp"], outs["vfp"] = heads_t(kp), heads_t(vp)
            outs["lfp"] = lfp.reshape(batch, seq, N_HEADS)[None]
            outs["kfs"], outs["vfs"] = heads(ks, (dec_batch, t_new)), heads(vs, (dec_batch, t_new))
            outs["lfs"] = lfs.reshape(dec_batch, t_new, N_HEADS)[None]
        g, b = norm(i, 1)
        yp = _outproj_t(yp, mp, w_out, g, b, tm)
        ys = _outproj(ys, ms, w_out, g, b)
        yp, ys = ffn_both(yp, ys, i, 1)

    return (yp.reshape(batch, seq, d), ys.reshape(dec_batch, t_new, d),
            outs["kmp"], outs["vmp"], outs["kfp"], outs["vfp"], outs["lfp"],
            outs["kms"], outs["vms"], outs["kfs"], outs["vfs"], outs["lfs"])
```
